```python
import jax, jax.numpy as jnp
from jax import lax
import numpy as np

D_MODEL = 1024
BATCH = 8
SEQ = 4096
DEPTH = 4

GRID_W = 64
Q_BLOCK = 128
ROPE_THETA = 10000.0
EPS = 1e-6
N_HEADS_A = 8
N_KV_A = 2
GROUP_A = N_HEADS_A // N_KV_A
HEAD_DIM_A = 64
WIDTH_A = N_HEADS_A * HEAD_DIM_A
N_HEADS_B = 8
Q_LORA = 384
KV_LORA = 256
QK_NOPE = 64
QK_ROPE = 32
V_DIM_B = 64
WIDTH_B = N_HEADS_B * V_DIM_B
IN_SIZES = (WIDTH_A, N_KV_A * HEAD_DIM_A, N_KV_A * HEAD_DIM_A, Q_LORA, KV_LORA, QK_ROPE, D_MODEL, D_MODEL)
IN_COLS = sum(IN_SIZES)
N_MOD = 6
N_EXPERTS = 16
N_GROUPS = 4
EXPERTS_PER_GROUP = N_EXPERTS // N_GROUPS
TOP_K = 2
D_EXPERT = 512
MOE_BLOCK = 128

kernel_name = "hybrid_gqa_mla_grouped_moe_encoder"


def rms_norm(x, g):
    xf = x.astype(jnp.float32)
    y = xf * lax.rsqrt(jnp.mean(xf * xf, axis=-1, keepdims=True) + EPS)
    return (y * g.astype(jnp.float32)).astype(x.dtype)


def split_points(sizes):
    pts, acc = [], 0
    for s in sizes[:-1]:
        acc += s
        pts.append(acc)
    return pts


def axial_rope_tables(row, col, rot_dim):
    n = rot_dim // 4
    inv = 1.0 / (ROPE_THETA ** (jnp.arange(n, dtype=jnp.float32) / n))
    ang = jnp.concatenate([row[:, None] * inv, col[:, None] * inv], axis=-1)
    return jnp.cos(ang), jnp.sin(ang)


def apply_rope(x, cos, sin):
    half = x.shape[-1] // 2
    shape = (cos.shape[0],) + (1,) * (x.ndim - 3) + (half,)
    c = cos.reshape(shape).astype(x.dtype)
    s = sin.reshape(shape).astype(x.dtype)
    x1, x2 = x[..., :half], x[..., half:]
    return jnp.concatenate([x1 * c - x2 * s, x2 * c + x1 * s], axis=-1)


def gqa_attention(q, k, v):
    B, S, H, hd = q.shape
    nb = S // Q_BLOCK
    qb = q.reshape(B, nb, Q_BLOCK, N_KV_A, GROUP_A, hd).transpose(1, 0, 2, 3, 4, 5)
    scale = hd ** -0.5

    def block(qi):
        s = jnp.einsum('bqkgd,bskd->bkgqs', qi, k).astype(jnp.float32) * scale
        p = jax.nn.softmax(s, axis=-1).astype(v.dtype)
        return jnp.einsum('bkgqs,bskd->bqkgd', p, v)

    o = lax.map(block, qb)
    return o.transpose(1, 0, 2, 3, 4, 5).reshape(B, S, H * hd)


def mla_attention(q_nope, q_rope, k_nope, k_rope, v):
    B, S, H, dn = q_nope.shape
    dr = q_rope.shape[-1]
    nb = S // Q_BLOCK
    qn = q_nope.reshape(B, nb, Q_BLOCK, H, dn).swapaxes(0, 1)
    qr = q_rope.reshape(B, nb, Q_BLOCK, H, dr).swapaxes(0, 1)
    scale = (dn + dr) ** -0.5

    def block(args):
        qn_i, qr_i = args
        s = (jnp.einsum('bqhd,bshd->bhqs', qn_i, k_nope)
             + jnp.einsum('bqhr,bsr->bhqs', qr_i, k_rope)).astype(jnp.float32) * scale
        p = jax.nn.softmax(s, axis=-1).astype(v.dtype)
        return jnp.einsum('bhqs,bshd->bqhd', p, v)

    o = lax.map(block, (qn, qr))
    return o.swapaxes(0, 1).reshape(B, S, H * v.shape[-1])


def hybrid_mixer(h, w_in, q_norm_a, k_norm_a, q_a_norm, kv_norm, w_q_b, w_kv_b,
                 w_proj_a, w_proj_b, w_o, cos_a, sin_a, cos_b, sin_b):
    B, S, _ = h.shape
    z = h @ w_in
    q_a, k_a, v_a, q_lat, kv_lat, k_rope, gate_a, gate_b = jnp.split(z, split_points(IN_SIZES), axis=-1)

    qa = apply_rope(rms_norm(q_a.reshape(B, S, N_HEADS_A, HEAD_DIM_A), q_norm_a), cos_a, sin_a)
    ka = apply_rope(rms_norm(k_a.reshape(B, S, N_KV_A, HEAD_DIM_A), k_norm_a), cos_a, sin_a)
    va = v_a.reshape(B, S, N_KV_A, HEAD_DIM_A)
    y_a = gqa_attention(qa, ka, va) @ w_proj_a

    qb = (rms_norm(q_lat, q_a_norm) @ w_q_b).reshape(B, S, N_HEADS_B, QK_NOPE + QK_ROPE)
    qb_nope = qb[..., :QK_NOPE]
    qb_rope = apply_rope(qb[..., QK_NOPE:], cos_b, sin_b)
    kv = (rms_norm(kv_lat, kv_norm) @ w_kv_b).reshape(B, S, N_HEADS_B, QK_NOPE + V_DIM_B)
    k_nope, v_b = kv[..., :QK_NOPE], kv[..., QK_NOPE:]
    k_r = apply_rope(k_rope, cos_b, sin_b)
    y_b = mla_attention(qb_nope, qb_rope, k_nope, k_r, v_b) @ w_proj_b

    merged = jax.nn.sigmoid(gate_a) * y_a + jax.nn.sigmoid(gate_b) * y_b
    return merged @ w_o


def grouped_moe(h, w_router, router_bias, w1, w3, w2):
    T, D = h.shape
    scores = jax.nn.sigmoid((h @ w_router).astype(jnp.float32))
    biased = scores + router_bias.astype(jnp.float32)
    grp_score = lax.top_k(biased.reshape(T, N_GROUPS, EXPERTS_PER_GROUP), 2)[0].sum(-1)
    g_sel = jnp.argmax(grp_score, axis=-1)
    expert_group = jnp.arange(N_EXPERTS) // EXPERTS_PER_GROUP
    masked = jnp.where(expert_group[None, :] == g_sel[:, None], biased, -jnp.inf)
    _, expert = lax.top_k(masked, TOP_K)
    gw = jnp.take_along_axis(scores, expert, axis=1)
    gw = gw / jnp.sum(gw, axis=-1, keepdims=True)

    n_slots = T * TOP_K
    e_flat = expert.reshape(-1)
    tok_flat = jnp.repeat(jnp.arange(T), TOP_K)
    order = jnp.argsort(e_flat)
    e_sorted = e_flat[order]
    tok_sorted = tok_flat[order]
    g_sorted = gw.reshape(-1)[order]
    counts = jnp.bincount(e_flat, length=N_EXPERTS)
    padded = ((counts + MOE_BLOCK - 1) // MOE_BLOCK) * MOE_BLOCK
    starts = jnp.cumsum(counts) - counts
    pends = jnp.cumsum(padded)
    pstarts = pends - padded
    dest = pstarts[e_sorted] + (jnp.arange(n_slots) - starts[e_sorted])
    n_pad = ((n_slots + MOE_BLOCK - 1) // MOE_BLOCK) * MOE_BLOCK + N_EXPERTS * MOE_BLOCK
    n_blk = n_pad // MOE_BLOCK
    xs = jnp.zeros((n_pad, D), h.dtype).at[dest].set(h[tok_sorted])
    blk_e = jnp.clip(jnp.searchsorted(pends, jnp.arange(n_blk) * MOE_BLOCK, side='right'), 0, N_EXPERTS - 1)

    def expert_block(args):
        xb, e = args
        return (jax.nn.silu(xb @ w1[e]) * (xb @ w3[e])) @ w2[e]

    ys = lax.map(expert_block, (xs.reshape(n_blk, MOE_BLOCK, D), blk_e)).reshape(n_pad, D)
    return jnp.zeros((T, D), h.dtype).at[tok_sorted].add(ys[dest] * g_sorted[:, None].astype(h.dtype))


def setup_inputs(seed: int = 0) -> dict:
    key = jax.random.key(seed)
    ks = jax.random.split(key, 22)
    f32 = jnp.float32
    nrm = lambda k, shape, s: jax.random.normal(k, shape, f32) * s
    gain = lambda k, shape: 1.0 + 0.01 * jax.random.normal(k, shape, f32)
    return {
        "x": nrm(ks[0], (BATCH, SEQ, D_MODEL), 1.0),
        "c": nrm(ks[1], (BATCH, D_MODEL), 1.0),
        "w_ada": nrm(ks[2], (DEPTH, D_MODEL, N_MOD * D_MODEL), 0.5 * D_MODEL ** -0.5),
        "b_ada": nrm(ks[3], (DEPTH, N_MOD * D_MODEL), 0.01),
        "norm1": gain(ks[4], (DEPTH, D_MODEL)),
        "w_in": nrm(ks[5], (DEPTH, D_MODEL, IN_COLS), D_MODEL ** -0.5),
        "q_norm_a": gain(ks[6], (DEPTH, HEAD_DIM_A)),
        "k_norm_a": gain(ks[7], (DEPTH, HEAD_DIM_A)),
        "q_a_norm": gain(ks[8], (DEPTH, Q_LORA)),
        "kv_norm": gain(ks[9], (DEPTH, KV_LORA)),
        "w_q_b": nrm(ks[10], (DEPTH, Q_LORA, N_HEADS_B * (QK_NOPE + QK_ROPE)), Q_LORA ** -0.5),
        "w_kv_b": nrm(ks[11], (DEPTH, KV_LORA, N_HEADS_B * (QK_NOPE + V_DIM_B)), KV_LORA ** -0.5),
        "w_proj_a": nrm(ks[12], (DEPTH, WIDTH_A, D_MODEL), WIDTH_A ** -0.5),
        "w_proj_b": nrm(ks[13], (DEPTH, WIDTH_B, D_MODEL), WIDTH_B ** -0.5),
        "w_o": nrm(ks[14], (DEPTH, D_MODEL, D_MODEL), D_MODEL ** -0.5),
        "norm2": gain(ks[15], (DEPTH, D_MODEL)),
        "w_router": nrm(ks[16], (D_MODEL, N_EXPERTS), D_MODEL ** -0.5),
        "router_bias": nrm(ks[17], (N_EXPERTS,), 0.01),
        "w1": nrm(ks[18], (DEPTH, N_EXPERTS, D_MODEL, D_EXPERT), D_MODEL ** -0.5),
        "w3": nrm(ks[19], (DEPTH, N_EXPERTS, D_MODEL, D_EXPERT), D_MODEL ** -0.5),
        "w2": nrm(ks[20], (DEPTH, N_EXPERTS, D_EXPERT, D_MODEL), D_EXPERT ** -0.5),
        "norm_f": gain(ks[21], (D_MODEL,)),
    }


def reference(x, c, w_ada, b_ada, norm1, w_in, q_norm_a, k_norm_a, q_a_norm, kv_norm,
              w_q_b, w_kv_b, w_proj_a, w_proj_b, w_o, norm2, w_router, router_bias,
              w1, w3, w2, norm_f):
    B, S, D = x.shape
    rows = S // GRID_W
    row = jnp.broadcast_to(jnp.arange(rows, dtype=jnp.float32)[:, None], (rows, GRID_W)).reshape(S)
    col = jnp.broadcast_to(jnp.arange(GRID_W, dtype=jnp.float32)[None, :], (rows, GRID_W)).reshape(S)
    cos_a, sin_a = axial_rope_tables(row, col, HEAD_DIM_A)
    cos_b, sin_b = axial_rope_tables(row, col, QK_ROPE)
    cond = jax.nn.silu(c)

    for l in range(DEPTH):
        mod = (cond @ w_ada[l] + b_ada[l])[:, None, :]
        sh1, sc1, g1, sh2, sc2, g2 = jnp.split(mod, N_MOD, axis=-1)
        h = rms_norm(x, norm1[l]) * (1.0 + sc1) + sh1
        x = x + g1 * hybrid_mixer(h, w_in[l], q_norm_a[l], k_norm_a[l], q_a_norm[l], kv_norm[l],
                                  w_q_b[l], w_kv_b[l], w_proj_a[l], w_proj_b[l], w_o[l],
                                  cos_a, sin_a, cos_b, sin_b)
        h = rms_norm(x, norm2[l]) * (1.0 + sc2) + sh2
        x = x + g2 * grouped_moe(h.reshape(B * S, D), w_router, router_bias,
                                 w1[l], w3[l], w2[l]).reshape(B, S, D)

    return rms_norm(x, norm_f)
```

```python
import functools

import jax
import jax.numpy as jnp
from jax import lax
from jax.experimental import pallas as pl
from jax.experimental.pallas import tpu as pltpu

F32 = jnp.float32
BF16 = jnp.bfloat16

D_MODEL = 1024
GRID_W = 64
ROPE_THETA = 10000.0
EPS = 1e-6
N_HEADS_A = 8
N_KV_A = 2
GROUP_A = N_HEADS_A // N_KV_A
HEAD_DIM_A = 64
N_HEADS_B = 8
Q_LORA = 384
KV_LORA = 256
QK_NOPE = 64
QK_ROPE = 32
V_DIM_B = 64
N_MOD = 6
N_EXPERTS = 16
N_GROUPS = 4
EXPERTS_PER_GROUP = N_EXPERTS // N_GROUPS
D_EXPERT = 512

LANES = 128
VMEM_LIMIT = 56 * 1024 * 1024

_C_QA = 0
_C_KA = _C_QA + N_HEADS_A * LANES
_C_VA = _C_KA + N_KV_A * LANES
_C_QL = _C_VA + N_KV_A * LANES
_C_KVL = _C_QL + Q_LORA
_C_KR = _C_KVL + KV_LORA
_C_GA = _C_KR + LANES
_C_GB = _C_GA + D_MODEL
_C_END = _C_GB + D_MODEL


def _cparams(sem):
    return pltpu.CompilerParams(dimension_semantics=sem, vmem_limit_bytes=VMEM_LIMIT)


def _split(a):
    hi = a.astype(BF16)
    lo = (a - hi.astype(F32)).astype(BF16)
    return hi, lo


def _dot(a, b):
    return jnp.dot(a, b, preferred_element_type=F32)


def _dot3(a, b):
    ah, al = _split(a)
    bh, bl = _split(b)
    return _dot(ah, bh) + (_dot(ah, bl) + _dot(al, bh))


def _mod_kernel(c_ref, w_ref, b_ref, o_ref):
    c = c_ref[...]
    cond = c * jax.nn.sigmoid(c)
    o_ref[...] = _dot3(cond, w_ref[...]) + b_ref[...]


def _modulation(c, w_ada, b_ada):
    depth, d, n = w_ada.shape
    bsz = c.shape[0]
    cb = 1536
    return pl.pallas_call(
        _mod_kernel,
        out_shape=jax.ShapeDtypeStruct((depth, bsz, n), F32),
        grid=(depth, n // cb),
        in_specs=[
            pl.BlockSpec((bsz, d), lambda l, j: (0, 0)),
            pl.BlockSpec((None, d, cb), lambda l, j: (l, 0, j)),
            pl.BlockSpec((None, 1, cb), lambda l, j: (l, 0, j)),
        ],
        out_specs=pl.BlockSpec((None, bsz, cb), lambda l, j: (l, 0, j)),
        compiler_params=_cparams(("parallel", "parallel")),
        name="adaln_mod",
    )(c, w_ada, b_ada.reshape(depth, 1, n))


def _rms(v, n):
    return v * lax.rsqrt(jnp.sum(v * v, axis=-1, keepdims=True) * (1.0 / n) + EPS)


def _rope(v, cos, sin_lo, sin_hi, shift):
    return (v * cos + pltpu.roll(v, LANES - shift, 1) * sin_lo
            + pltpu.roll(v, shift, 1) * sin_hi)


def _inproj_kernel(x_ref, mod_ref, n1_ref, win_ref, gqa_ref, gka_ref, gql_ref, gkv_ref,
                   wq_ref, wk_ref, wv_ref, ta_ref, tb_ref,
                   qa_ref, ka_ref, va_ref, qb_ref, kb_ref, vb_ref, ga_ref, gb_ref):
    x = x_ref[...]
    sh1 = mod_ref[0:1, :]
    sc1 = mod_ref[1:2, :]
    h = (_rms(x, D_MODEL) * n1_ref[...]) * (1.0 + sc1) + sh1
    hb = h.astype(BF16)

    cos_a, sin_a_lo, sin_a_hi = ta_ref[0], ta_ref[1], ta_ref[2]
    cos_b, sin_b_lo, sin_b_hi = tb_ref[0], tb_ref[1], tb_ref[2]
    cos_k, sin_k_lo, sin_k_hi = tb_ref[3], tb_ref[4], tb_ref[5]

    gqa = gqa_ref[...]
    for hh in range(N_HEADS_A):
        z = _dot(hb, win_ref[:, _C_QA + hh * LANES:_C_QA + (hh + 1) * LANES])
        zn = _rms(z, HEAD_DIM_A) * gqa
        qa_ref[:, hh * LANES:(hh + 1) * LANES] = _rope(
            zn, cos_a, sin_a_lo, sin_a_hi, HEAD_DIM_A // 2).astype(BF16)
    gka = gka_ref[...]
    for hh in range(N_KV_A):
        z = _dot(hb, win_ref[:, _C_KA + hh * LANES:_C_KA + (hh + 1) * LANES])
        zn = _rms(z, HEAD_DIM_A) * gka
        ka_ref[:, hh * LANES:(hh + 1) * LANES] = _rope(
            zn, cos_a, sin_a_lo, sin_a_hi, HEAD_DIM_A // 2).astype(BF16)
    va_ref[...] = _dot(hb, win_ref[:, _C_VA:_C_QL]).astype(BF16)

    ql = _dot(hb, win_ref[:, _C_QL:_C_KVL])
    qlb = (_rms(ql, Q_LORA) * gql_ref[...]).astype(BF16)
    for hh in range(N_HEADS_B):
        z = _dot(qlb, wq_ref[:, hh * LANES:(hh + 1) * LANES])
        qb_ref[:, hh * LANES:(hh + 1) * LANES] = _rope(
            z, cos_b, sin_b_lo, sin_b_hi, QK_ROPE // 2).astype(BF16)

    kvl = _dot(hb, win_ref[:, _C_KVL:_C_KR])
    kvb = (_rms(kvl, KV_LORA) * gkv_ref[...]).astype(BF16)
    kr = _dot(hb, win_ref[:, _C_KR:_C_GA])
    kr = _rope(kr, cos_k, sin_k_lo, sin_k_hi, QK_ROPE // 2)
    for hh in range(N_HEADS_B):
        z = _dot(kvb, wk_ref[:, hh * LANES:(hh + 1) * LANES])
        kb_ref[:, hh * LANES:(hh + 1) * LANES] = (z + kr).astype(BF16)
    vb_ref[...] = _dot(kvb, wv_ref[...]).astype(BF16)

    ga_ref[...] = jax.nn.sigmoid(_dot(hb, win_ref[:, _C_GA:_C_GB])).astype(BF16)
    gb_ref[...] = jax.nn.sigmoid(_dot(hb, win_ref[:, _C_GB:_C_END])).astype(BF16)


def _inproj(x, mod_l, n1, win, gqa, gka, gql, gkv, wq, wk, wv, ta, tb, tm):
    bsz, seq, d = x.shape
    nt = seq // tm
    row = lambda w: pl.BlockSpec((None, tm, w), lambda b, i: (b, i, 0))
    full = lambda a: pl.BlockSpec(a.shape, lambda b, i: (0,) * a.ndim)
    widths = (N_HEADS_A * LANES, N_KV_A * LANES, N_KV_A * LANES,
              N_HEADS_B * LANES, N_HEADS_B * LANES, N_HEADS_B * LANES, D_MODEL, D_MODEL)
    return pl.pallas_call(
        _inproj_kernel,
        out_shape=[jax.ShapeDtypeStruct((bsz, seq, w), BF16) for w in widths],
        grid=(bsz, nt),
        in_specs=[
            row(d),
            pl.BlockSpec((None, N_MOD, d), lambda b, i: (b, 0, 0)),
            full(n1), full(win), full(gqa), full(gka), full(gql), full(gkv),
            full(wq), full(wk), full(wv),
            pl.BlockSpec((3, tm, LANES), lambda b, i: (0, i, 0)),
            pl.BlockSpec((6, tm, LANES), lambda b, i: (0, i, 0)),
        ],
        out_specs=[row(w) for w in widths],
        compiler_params=_cparams(("parallel", "parallel")),
        name="inproj",
    )(x, mod_l, n1, win, gqa, gka, gql, gkv, wq, wk, wv, ta, tb)


def _attn_kernel(q_ref, k_ref, v_ref, o_ref, *, group, tk):
    tq = q_ref.shape[0]
    seq = k_ref.shape[0]
    if group == 1:
        q = q_ref[...]
    else:
        q = jnp.concatenate(
            [q_ref[:, g * LANES:(g + 1) * LANES] for g in range(group)], axis=0)
    m_rows = group * tq

    def body(j, carry):
        m, l, acc = carry
        start = pl.multiple_of(j * tk, tk)
        k = k_ref[pl.ds(start, tk), :]
        v = v_ref[pl.ds(start, tk), :]
        s = lax.dot_general(q, k, (((1,), (1,)), ((), ())), preferred_element_type=F32)
        m_new = jnp.maximum(m, jnp.max(s, axis=-1, keepdims=True))
        alpha = jnp.exp(m - m_new)
        p = jnp.exp(s - m_new)
        l = alpha * l + jnp.sum(p, axis=-1, keepdims=True)
        acc = alpha * acc + _dot(p.astype(BF16), v)
        return m_new, l, acc

    init = (jnp.full((m_rows, 1), -jnp.inf, F32), jnp.zeros((m_rows, 1), F32),
            jnp.zeros((m_rows, LANES), F32))
    _, l, acc = lax.fori_loop(0, seq // tk, body, init)
    o = (acc / l).astype(o_ref.dtype)
    for g in range(group):
        o_ref[:, g * LANES:(g + 1) * LANES] = o[g * tq:(g + 1) * tq, :]


def _attention(q, k, v, *, group, tq, tk):
    bsz, seq, qw = q.shape
    n_kv = k.shape[-1] // LANES
    return pl.pallas_call(
        functools.partial(_attn_kernel, group=group, tk=tk),
        out_shape=jax.ShapeDtypeStruct((bsz, seq, qw), BF16),
        grid=(bsz, n_kv, seq // tq),
        in_specs=[
            pl.BlockSpec((None, tq, group * LANES), lambda b, h, i: (b, i, h)),
            pl.BlockSpec((None, seq, LANES), lambda b, h, i: (b, 0, h)),
            pl.BlockSpec((None, seq, LANES), lambda b, h, i: (b, 0, h)),
        ],
        out_specs=pl.BlockSpec((None, tq, group * LANES), lambda b, h, i: (b, i, h)),
        compiler_params=_cparams(("parallel", "parallel", "parallel")),
        name="flash_attn_g%d" % group,
    )(q, k, v)


def _first_argmax(vals):
    best, idx = vals[0], jnp.zeros_like(vals[0], dtype=jnp.int32)
    for i in range(1, len(vals)):
        gt = vals[i] > best
        best = jnp.where(gt, vals[i], best)
        idx = jnp.where(gt, i, idx)
    return best, idx


def _pick(vals, idx):
    out = vals[0]
    for i in range(1, len(vals)):
        out = jnp.where(idx == i, vals[i], out)
    return out


def _outproj_kernel(oa_ref, ob_ref, ga_ref, gb_ref, x_ref, mod_ref, n2_ref,
                    wpa_ref, wpb_ref, wo_ref, wr_ref, rb_ref,
                    x1_ref, h2_ref, eid_ref, rank_ref, gw_ref, cnt_ref, base_ref):
    tm = x_ref.shape[0]

    @pl.when((pl.program_id(0) == 0) & (pl.program_id(1) == 0))
    def _():
        base_ref[...] = jnp.zeros_like(base_ref)

    ya = _dot(oa_ref[...], wpa_ref[...])
    yb = _dot(ob_ref[...], wpb_ref[...])
    merged = ga_ref[...].astype(F32) * ya + gb_ref[...].astype(F32) * yb
    att = _dot(merged.astype(BF16), wo_ref[...])
    g1 = mod_ref[2:3, :]
    sh2 = mod_ref[3:4, :]
    sc2 = mod_ref[4:5, :]
    x1 = x_ref[...] + g1 * att
    x1_ref[...] = x1
    h2 = (_rms(x1, D_MODEL) * n2_ref[...]) * (1.0 + sc2) + sh2
    h2_ref[...] = h2

    logits = _dot3(h2, wr_ref[...])
    lt = logits.T[0:N_EXPERTS, :]
    scores = jax.nn.sigmoid(lt)
    biased = scores + rb_ref[...]
    brow = [biased[e:e + 1, :] for e in range(N_EXPERTS)]
    srow = [scores[e:e + 1, :] for e in range(N_EXPERTS)]
    grp = []
    for g in range(N_GROUPS):
        a, b, c, d = brow[4 * g:4 * g + 4]
        hi1, lo1 = jnp.maximum(a, b), jnp.minimum(a, b)
        hi2, lo2 = jnp.maximum(c, d), jnp.minimum(c, d)
        grp.append(jnp.maximum(hi1, hi2) + jnp.maximum(jnp.minimum(hi1, hi2),
                                                       jnp.maximum(lo1, lo2)))
    _, gsel = _first_argmax(grp)
    bsel = [_pick([brow[4 * g + i] for g in range(N_GROUPS)], gsel)
            for i in range(EXPERTS_PER_GROUP)]
    ssel = [_pick([srow[4 * g + i] for g in range(N_GROUPS)], gsel)
            for i in range(EXPERTS_PER_GROUP)]
    _, i0 = _first_argmax(bsel)
    _, i1 = _first_argmax([jnp.where(i0 == i, -jnp.inf, bsel[i])
                           for i in range(EXPERTS_PER_GROUP)])
    w0 = _pick(ssel, i0)
    w1 = _pick(ssel, i1)
    wsum = w0 + w1
    e0 = gsel * EXPERTS_PER_GROUP + i0
    e1 = gsel * EXPERTS_PER_GROUP + i1
    eid_ref[0:1, :] = e0
    eid_ref[1:2, :] = e1
    gw_ref[0:1, :] = w0 / wsum
    gw_ref[1:2, :] = w1 / wsum

    eio = lax.broadcasted_iota(jnp.int32, (N_EXPERTS, tm), 0)
    oh0 = eio == e0
    oh1 = eio == e1
    cnt = jnp.where(oh0 | oh1, 1.0, 0.0)
    r_i = lax.broadcasted_iota(jnp.int32, (tm, tm), 0)
    c_i = lax.broadcasted_iota(jnp.int32, (tm, tm), 1)
    upper = jnp.where(r_i < c_i, 1.0, 0.0).astype(BF16)
    pref = _dot(cnt.astype(BF16), upper) + base_ref[:, 0:1]
    rank_ref[0:1, :] = jnp.sum(jnp.where(oh0, pref, 0.0), axis=0, keepdims=True).astype(jnp.int32)
    rank_ref[1:2, :] = jnp.sum(jnp.where(oh1, pref, 0.0), axis=0, keepdims=True).astype(jnp.int32)
    base_ref[...] = base_ref[...] + jnp.sum(cnt, axis=1, keepdims=True)
    cnt_ref[...] = base_ref[...]


def _outproj(oa, ob, ga, gb, x, mod_l, n2, wpa, wpb, wo, wr, rb, tm):
    bsz, seq, d = x.shape
    nt = seq // tm
    row = lambda w: pl.BlockSpec((None, tm, w), lambda b, i: (b, i, 0))
    full = lambda a: pl.BlockSpec(a.shape, lambda b, i: (0,) * a.ndim)
    tok = pl.BlockSpec((2, tm), lambda b, i: (0, b * nt + i))
    t = bsz * seq
    return pl.pallas_call(
        _outproj_kernel,
        out_shape=[
            jax.ShapeDtypeStruct((bsz, seq, d), F32),
            jax.ShapeDtypeStruct((bsz, seq, d), F32),
            jax.ShapeDtypeStruct((2, t), jnp.int32),
            jax.ShapeDtypeStruct((2, t), jnp.int32),
            jax.ShapeDtypeStruct((2, t), F32),
            jax.ShapeDtypeStruct((N_EXPERTS, LANES), F32),
        ],
        grid=(bsz, nt),
        in_specs=[
            row(d), row(d), row(d), row(d), row(d),
            pl.BlockSpec((None, N_MOD, d), lambda b, i: (b, 0, 0)),
            full(n2), full(wpa), full(wpb), full(wo), full(wr), full(rb),
        ],
        out_specs=[row(d), row(d), tok, tok, tok,
                   pl.BlockSpec((N_EXPERTS, LANES), lambda b, i: (0, 0))],
        scratch_shapes=[pltpu.VMEM((N_EXPERTS, LANES), F32)],
        compiler_params=_cparams(("arbitrary", "arbitrary")),
        name="outproj_router",
    )(oa, ob, ga, gb, x, mod_l, n2, wpa, wpb, wo, wr, rb)


def _dispatch_kernel(dest_ref, h_ref, xs_ref, sem):
    tm = h_ref.shape[0]

    def issue(t, c):
        pltpu.make_async_copy(h_ref.at[pl.ds(t, 1)], xs_ref.at[pl.ds(dest_ref[0, t], 1)],
                              sem).start()
        pltpu.make_async_copy(h_ref.at[pl.ds(t, 1)], xs_ref.at[pl.ds(dest_ref[0, tm + t], 1)],
                              sem).start()
        return c

    lax.fori_loop(0, tm, issue, 0)
    for _ in range(2):
        pltpu.make_async_copy(h_ref, xs_ref.at[pl.ds(0, tm)], sem).wait()


def _dispatch(h2, dest, n_rows, tm):
    t, d = h2.shape
    nt = t // tm
    dest_t = jnp.concatenate([dest[0].reshape(nt, 1, tm), dest[1].reshape(nt, 1, tm)], axis=-1)
    return pl.pallas_call(
        _dispatch_kernel,
        out_shape=jax.ShapeDtypeStruct((n_rows, d), F32),
        grid=(nt,),
        in_specs=[
            pl.BlockSpec((None, 1, 2 * tm), lambda i: (i, 0, 0), memory_space=pltpu.SMEM),
            pl.BlockSpec((tm, d), lambda i: (i, 0)),
        ],
        out_specs=pl.BlockSpec(memory_space=pl.ANY),
        scratch_shapes=[pltpu.SemaphoreType.DMA],
        compiler_params=_cparams(("arbitrary",)),
        name="moe_dispatch",
    )(dest_t, h2)


def _expert_kernel(be_ref, br_ref, nv_ref, x_ref, w1_ref, w3_ref, w2_ref, y_ref):
    b = pl.program_id(0)

    @pl.when(b < nv_ref[0])
    def _():
        bm = x_ref.shape[0]
        rows = lax.broadcasted_iota(jnp.int32, (bm, 1), 0)
        x = jnp.where(rows < br_ref[b], x_ref[...], 0.0).astype(BF16)
        a = _dot(x, w1_ref[...])
        g = _dot(x, w3_ref[...])
        hmid = (a * jax.nn.sigmoid(a)) * g
        y_ref[...] = _dot(hmid.astype(BF16), w2_ref[...])


def _experts(xs, w1, w3, w2, blk_e, blk_rows, n_valid, bm):
    n_rows, d = xs.shape
    nb = n_rows // bm
    f = w1.shape[-1]
    last = lambda b, nv: jnp.minimum(b, nv[0] - 1)
    return pl.pallas_call(
        _expert_kernel,
        out_shape=jax.ShapeDtypeStruct((n_rows, d), F32),
        grid_spec=pltpu.PrefetchScalarGridSpec(
            num_scalar_prefetch=3,
            grid=(nb,),
            in_specs=[
                pl.BlockSpec((bm, d), lambda b, be, br, nv: (last(b, nv), 0)),
                pl.BlockSpec((None, d, f), lambda b, be, br, nv: (be[last(b, nv)], 0, 0)),
                pl.BlockSpec((None, d, f), lambda b, be, br, nv: (be[last(b, nv)], 0, 0)),
                pl.BlockSpec((None, f, d), lambda b, be, br, nv: (be[last(b, nv)], 0, 0)),
            ],
            out_specs=pl.BlockSpec((bm, d), lambda b, be, br, nv: (last(b, nv), 0)),
        ),
        compiler_params=_cparams(("arbitrary",)),
        name="moe_experts",
    )(blk_e, blk_rows, n_valid, xs, w1, w3, w2)


def _combine_kernel(dest_ref, x1_ref, gw_ref, mod_ref, nf_ref, ys_ref, o_ref, buf, sem, *, final):
    tm = x1_ref.shape[0]

    def issue(t, c):
        pltpu.make_async_copy(ys_ref.at[pl.ds(dest_ref[0, t], 1)], buf.at[pl.ds(t, 1)],
                              sem).start()
        pltpu.make_async_copy(ys_ref.at[pl.ds(dest_ref[0, tm + t], 1)],
                              buf.at[pl.ds(tm + t, 1)], sem).start()
        return c

    lax.fori_loop(0, tm, issue, 0)
    pltpu.make_async_copy(ys_ref.at[pl.ds(0, 2 * tm)], buf, sem).wait()
    gw = gw_ref[...]
    moe = gw[:, 0:1] * buf[0:tm, :] + gw[:, 1:2] * buf[tm:2 * tm, :]
    x2 = x1_ref[...] + mod_ref[5:6, :] * moe
    if final:
        x2 = _rms(x2, D_MODEL) * nf_ref[...]
    o_ref[...] = x2


def _combine(ys, dest, x1, gw_t, mod_l, nf, tm, final):
    bsz, seq, d = x1.shape
    nt = seq // tm
    ntt = bsz * nt
    dest_t = jnp.concatenate([dest[0].reshape(ntt, 1, tm), dest[1].reshape(ntt, 1, tm)], axis=-1)
    return pl.pallas_call(
        functools.partial(_combine_kernel, final=final),
        out_shape=jax.ShapeDtypeStruct((bsz, seq, d), F32),
        grid=(bsz, nt),
        in_specs=[
            pl.BlockSpec((None, 1, 2 * tm), lambda b, i: (b * nt + i, 0, 0),
                         memory_space=pltpu.SMEM),
            pl.BlockSpec((None, tm, d), lambda b, i: (b, i, 0)),
            pl.BlockSpec((tm, 2), lambda b, i: (b * nt + i, 0)),
            pl.BlockSpec((None, N_MOD, d), lambda b, i: (b, 0, 0)),
            pl.BlockSpec((1, d), lambda b, i: (0, 0)),
            pl.BlockSpec(memory_space=pl.ANY),
        ],
        out_specs=pl.BlockSpec((None, tm, d), lambda b, i: (b, i, 0)),
        scratch_shapes=[pltpu.VMEM((2 * tm, d), F32), pltpu.SemaphoreType.DMA],
        compiler_params=_cparams(("arbitrary", "arbitrary")),
        name="moe_combine",
    )(dest_t, x1, gw_t, mod_l, nf, ys)


def _rope_tables(seq):
    rows = seq // GRID_W
    row = jnp.broadcast_to(jnp.arange(rows, dtype=F32)[:, None], (rows, GRID_W)).reshape(seq)
    col = jnp.broadcast_to(jnp.arange(GRID_W, dtype=F32)[None, :], (rows, GRID_W)).reshape(seq)

    def cs(rot_dim):
        n = rot_dim // 4
        inv = 1.0 / (ROPE_THETA ** (jnp.arange(n, dtype=F32) / n))
        ang = jnp.concatenate([row[:, None] * inv, col[:, None] * inv], axis=-1)
        return jnp.cos(ang), jnp.sin(ang)

    def place(pieces):
        out = jnp.zeros((seq, LANES), F32)
        for off, val in pieces:
            out = out.at[:, off:off + val.shape[1]].set(val)
        return out

    ca, sa = cs(HEAD_DIM_A)
    ha = HEAD_DIM_A // 2
    scale_a = HEAD_DIM_A ** -0.5
    ta = jnp.stack([place([(0, ca), (ha, ca)]), place([(0, -sa)]), place([(ha, sa)])]) * scale_a
    cb, sb = cs(QK_ROPE)
    hb = QK_ROPE // 2
    scale_b = (QK_NOPE + QK_ROPE) ** -0.5
    ones = jnp.ones((seq, QK_NOPE), F32)
    cos_q = place([(0, ones), (QK_NOPE, cb), (QK_NOPE + hb, cb)]) * scale_b
    sin_lo = place([(QK_NOPE, -sb)])
    sin_hi = place([(QK_NOPE + hb, sb)])
    cos_k = place([(QK_NOPE, cb), (QK_NOPE + hb, cb)])
    tb = jnp.stack([cos_q, sin_lo * scale_b, sin_hi * scale_b, cos_k, sin_lo, sin_hi])
    return ta, tb


def _pad_heads(w, n_heads, width, offset=0):
    k = w.shape[0]
    w = w.reshape(k, n_heads, width)
    out = jnp.zeros((k, n_heads, LANES), w.dtype).at[:, :, offset:offset + width].set(w)
    return out.reshape(k, n_heads * LANES)


def _prep_layer(w_in, w_q_b, w_kv_b, w_proj_a, w_proj_b):
    o = 0
    parts = []
    sizes = (N_HEADS_A * HEAD_DIM_A, N_KV_A * HEAD_DIM_A, N_KV_A * HEAD_DIM_A,
             Q_LORA, KV_LORA, QK_ROPE, D_MODEL, D_MODEL)
    segs = []
    for s in sizes:
        segs.append(w_in[:, o:o + s])
        o += s
    parts = [
        _pad_heads(segs[0], N_HEADS_A, HEAD_DIM_A),
        _pad_heads(segs[1], N_KV_A, HEAD_DIM_A),
        _pad_heads(segs[2], N_KV_A, HEAD_DIM_A),
        segs[3], segs[4],
        _pad_heads(segs[5], 1, QK_ROPE, offset=QK_NOPE),
        segs[6], segs[7],
    ]
    win = jnp.concatenate(parts, axis=1).astype(BF16)
    wq = _pad_heads(w_q_b, N_HEADS_B, QK_NOPE + QK_ROPE).astype(BF16)
    wkv = w_kv_b.reshape(KV_LORA, N_HEADS_B, QK_NOPE + V_DIM_B)
    wk = _pad_heads(wkv[:, :, :QK_NOPE].reshape(KV_LORA, -1), N_HEADS_B, QK_NOPE).astype(BF16)
    wv = _pad_heads(wkv[:, :, QK_NOPE:].reshape(KV_LORA, -1), N_HEADS_B, V_DIM_B).astype(BF16)

    def pad_rows(w, n_heads, width):
        return _pad_heads(w.T, n_heads, width).T.astype(BF16)

    wpa = pad_rows(w_proj_a, N_HEADS_A, HEAD_DIM_A)
    wpb = pad_rows(w_proj_b, N_HEADS_B, V_DIM_B)
    return win, wq, wk, wv, wpa, wpb


def _pad_lanes(g, offset=0):
    return jnp.zeros((1, LANES), F32).at[0, offset:offset + g.shape[0]].set(g)


def kernel(x, c, w_ada, b_ada, norm1, w_in, q_norm_a, k_norm_a, q_a_norm, kv_norm, w_q_b, w_kv_b, w_proj_a, w_proj_b, w_o, norm2, w_router, router_bias, w1, w3, w2, norm_f):
    bsz, seq, d = x.shape
    depth = w_in.shape[0]
    t = bsz * seq
    tm = min(256, seq)
    bm = min(512, t)
    tk = min(512, seq)

    mod = _modulation(c, w_ada, b_ada).reshape(depth, bsz, N_MOD, d)
    ta, tb = _rope_tables(seq)
    wr = jnp.zeros((d, LANES), F32).at[:, :N_EXPERTS].set(w_router)
    rb = router_bias.reshape(N_EXPERTS, 1)
    n_rows = ((t * 2 + bm - 1) // bm) * bm + N_EXPERTS * bm
    nb = n_rows // bm
    w1b, w3b, w2b = w1.astype(BF16), w3.astype(BF16), w2.astype(BF16)
    nf = norm_f.reshape(1, d)

    for l in range(depth):
        win, wq, wk, wv, wpa, wpb = _prep_layer(w_in[l], w_q_b[l], w_kv_b[l],
                                                w_proj_a[l], w_proj_b[l])
        qa, ka, va, qb, kb, vb, ga, gb = _inproj(
            x, mod[l], norm1[l].reshape(1, d), win,
            _pad_lanes(q_norm_a[l]), _pad_lanes(k_norm_a[l]),
            q_a_norm[l].reshape(1, Q_LORA), kv_norm[l].reshape(1, KV_LORA),
            wq, wk, wv, ta, tb, tm)
        oa = _attention(qa, ka, va, group=GROUP_A, tq=min(128, seq), tk=tk)
        ob = _attention(qb, kb, vb, group=1, tq=min(512, seq), tk=tk)
        x1, h2, eid, rank, gw, cnt = _outproj(
            oa, ob, ga, gb, x, mod[l], norm2[l].reshape(1, d), wpa, wpb,
            w_o[l].astype(BF16), wr, rb, tm)

        counts = cnt[:, 0].astype(jnp.int32)
        padded = ((counts + bm - 1) // bm) * bm
        pends = jnp.cumsum(padded)
        pstarts = pends - padded
        dest = pstarts[eid] + rank
        blk_start = jnp.arange(nb, dtype=jnp.int32) * bm
        blk_e = jnp.clip(jnp.searchsorted(pends, blk_start, side='right'),
                         0, N_EXPERTS - 1).astype(jnp.int32)
        blk_rows = jnp.clip(pstarts[blk_e] + counts[blk_e] - blk_start, 0, bm).astype(jnp.int32)
        n_valid = (pends[-1:] // bm).astype(jnp.int32)

        xs = _dispatch(h2.reshape(t, d), dest, n_rows, tm)
        ys = _experts(xs, w1b[l], w3b[l], w2b[l], blk_e, blk_rows, n_valid, bm)
        x = _combine(ys, dest, x1, gw.T, mod[l], nf, tm, final=(l == depth - 1))
    return x
```

```python
import functools

import jax
import jax.numpy as jnp
from jax import lax
from jax.experimental import pallas as pl
from jax.experimental.pallas import tpu as pltpu

F32 = jnp.float32
BF16 = jnp.bfloat16

D_MODEL = 1024
GRID_W = 64
ROPE_THETA = 10000.0
EPS = 1e-6
N_HEADS_A = 8
N_KV_A = 2
GROUP_A = N_HEADS_A // N_KV_A
HEAD_DIM_A = 64
N_HEADS_B = 8
Q_LORA = 384
KV_LORA = 256
QK_NOPE = 64
QK_ROPE = 32
V_DIM_B = 64
N_MOD = 6
N_EXPERTS = 16
N_GROUPS = 4
EXPERTS_PER_GROUP = N_EXPERTS // N_GROUPS
D_EXPERT = 512
N_PAIRS = EXPERTS_PER_GROUP * (EXPERTS_PER_GROUP - 1) // 2
N_BUCKETS = N_GROUPS * N_PAIRS
PAIR_LO = (0, 0, 0, 1, 1, 2)
PAIR_HI = (1, 2, 3, 2, 3, 3)

LANES = 128
ONES_LANE = 64
LOG2E = 1.4426950408889634
XS_COLS = D_MODEL + LANES
DMA_UNROLL = 8
VMEM_LIMIT = 56 * 1024 * 1024

_C_QA = 0
_C_KA = _C_QA + N_HEADS_A * LANES
_C_VA = _C_KA + N_KV_A * LANES
_C_QL = _C_VA + N_KV_A * LANES
_C_KVL = _C_QL + Q_LORA
_C_KR = _C_KVL + KV_LORA
_C_GA = _C_KR + LANES
_C_GB = _C_GA + D_MODEL
_C_END = _C_GB + D_MODEL


def _cparams(sem):
    return pltpu.CompilerParams(dimension_semantics=sem, vmem_limit_bytes=VMEM_LIMIT)


def _split(a):
    hi = a.astype(BF16)
    lo = (a - hi.astype(F32)).astype(BF16)
    return hi, lo


def _dot(a, b):
    return jnp.dot(a, b, preferred_element_type=F32)


def _dot3(a, b):
    ah, al = _split(a)
    bh, bl = _split(b)
    return _dot(ah, bh) + (_dot(ah, bl) + _dot(al, bh))


def _mod_kernel(c_ref, w_ref, b_ref, o_ref):
    c = c_ref[...]
    cond = c * jax.nn.sigmoid(c)
    o_ref[...] = _dot3(cond, w_ref[...]) + b_ref[...]


def _modulation(c, w_ada, b_ada):
    depth, d, n = w_ada.shape
    bsz = c.shape[0]
    cb = 1536
    return pl.pallas_call(
        _mod_kernel,
        out_shape=jax.ShapeDtypeStruct((depth, bsz, n), F32),
        grid=(depth, n // cb),
        in_specs=[
            pl.BlockSpec((bsz, d), lambda l, j: (0, 0)),
            pl.BlockSpec((None, d, cb), lambda l, j: (l, 0, j)),
            pl.BlockSpec((None, 1, cb), lambda l, j: (l, 0, j)),
        ],
        out_specs=pl.BlockSpec((None, bsz, cb), lambda l, j: (l, 0, j)),
        compiler_params=_cparams(("parallel", "parallel")),
        name="adaln_mod",
    )(c, w_ada, b_ada.reshape(depth, 1, n))


def _rms(v, n):
    return v * lax.rsqrt(jnp.sum(v * v, axis=-1, keepdims=True) * (1.0 / n) + EPS)


def _rope(v, cos, sin_lo, sin_hi, shift):
    return (v * cos + pltpu.roll(v, LANES - shift, 1) * sin_lo
            + pltpu.roll(v, shift, 1) * sin_hi)


def _ones_lane(width):
    lane = lax.broadcasted_iota(jnp.int32, (1, width), 1)
    return jnp.where(lane % LANES == ONES_LANE, 1.0, 0.0)


def _inproj_kernel(x_ref, mod_ref, n1_ref, win_ref, gqa_ref, gka_ref, gql_ref, gkv_ref,
                   wq_ref, wk_ref, wv_ref, ta_ref, tb_ref,
                   qa_ref, ka_ref, va_ref, qb_ref, kb_ref, vb_ref, ga_ref, gb_ref):
    x = x_ref[...]
    sh1 = mod_ref[0:1, :]
    sc1 = mod_ref[1:2, :]
    h = (_rms(x, D_MODEL) * n1_ref[...]) * (1.0 + sc1) + sh1
    hb = h.astype(BF16)

    cos_a, sin_a_lo, sin_a_hi = ta_ref[0], ta_ref[1], ta_ref[2]
    cos_b, sin_b_lo, sin_b_hi = tb_ref[0], tb_ref[1], tb_ref[2]
    cos_k, sin_k_lo, sin_k_hi = tb_ref[3], tb_ref[4], tb_ref[5]

    gqa = gqa_ref[...]
    for hh in range(N_HEADS_A):
        z = _dot(hb, win_ref[:, _C_QA + hh * LANES:_C_QA + (hh + 1) * LANES])
        zn = _rms(z, HEAD_DIM_A) * gqa
        qa_ref[:, hh * LANES:(hh + 1) * LANES] = _rope(
            zn, cos_a, sin_a_lo, sin_a_hi, HEAD_DIM_A // 2).astype(BF16)
    gka = gka_ref[...]
    for hh in range(N_KV_A):
        z = _dot(hb, win_ref[:, _C_KA + hh * LANES:_C_KA + (hh + 1) * LANES])
        zn = _rms(z, HEAD_DIM_A) * gka
        ka_ref[:, hh * LANES:(hh + 1) * LANES] = _rope(
            zn, cos_a, sin_a_lo, sin_a_hi, HEAD_DIM_A // 2).astype(BF16)
    va_ref[...] = (_dot(hb, win_ref[:, _C_VA:_C_QL]) + _ones_lane(N_KV_A * LANES)).astype(BF16)

    ql = _dot(hb, win_ref[:, _C_QL:_C_KVL])
    qlb = (_rms(ql, Q_LORA) * gql_ref[...]).astype(BF16)
    for hh in range(N_HEADS_B):
        z = _dot(qlb, wq_ref[:, hh * LANES:(hh + 1) * LANES])
        qb_ref[:, hh * LANES:(hh + 1) * LANES] = _rope(
            z, cos_b, sin_b_lo, sin_b_hi, QK_ROPE // 2).astype(BF16)

    kvl = _dot(hb, win_ref[:, _C_KVL:_C_KR])
    kvb = (_rms(kvl, KV_LORA) * gkv_ref[...]).astype(BF16)
    kr = _dot(hb, win_ref[:, _C_KR:_C_GA])
    kr = _rope(kr, cos_k, sin_k_lo, sin_k_hi, QK_ROPE // 2)
    for hh in range(N_HEADS_B):
        z = _dot(kvb, wk_ref[:, hh * LANES:(hh + 1) * LANES])
        kb_ref[:, hh * LANES:(hh + 1) * LANES] = (z + kr).astype(BF16)
    vb_ref[...] = (_dot(kvb, wv_ref[...]) + _ones_lane(N_HEADS_B * LANES)).astype(BF16)

    ga_ref[...] = jax.nn.sigmoid(_dot(hb, win_ref[:, _C_GA:_C_GB])).astype(BF16)
    gb_ref[...] = jax.nn.sigmoid(_dot(hb, win_ref[:, _C_GB:_C_END])).astype(BF16)


def _inproj(x, mod_l, n1, win, gqa, gka, gql, gkv, wq, wk, wv, ta, tb, tm):
    bsz, seq, d = x.shape
    nt = seq // tm
    row = lambda w: pl.BlockSpec((None, tm, w), lambda b, i: (b, i, 0))
    full = lambda a: pl.BlockSpec(a.shape, lambda b, i: (0,) * a.ndim)
    widths = (N_HEADS_A * LANES, N_KV_A * LANES, N_KV_A * LANES,
              N_HEADS_B * LANES, N_HEADS_B * LANES, N_HEADS_B * LANES, D_MODEL, D_MODEL)
    return pl.pallas_call(
        _inproj_kernel,
        out_shape=[jax.ShapeDtypeStruct((bsz, seq, w), BF16) for w in widths],
        grid=(bsz, nt),
        in_specs=[
            row(d),
            pl.BlockSpec((None, N_MOD, d), lambda b, i: (b, 0, 0)),
            full(n1), full(win), full(gqa), full(gka), full(gql), full(gkv),
            full(wq), full(wk), full(wv),
            pl.BlockSpec((3, tm, LANES), lambda b, i: (0, i, 0)),
            pl.BlockSpec((6, tm, LANES), lambda b, i: (0, i, 0)),
        ],
        out_specs=[row(w) for w in widths],
        compiler_params=_cparams(("parallel", "parallel")),
        name="inproj",
    )(x, mod_l, n1, win, gqa, gka, gql, gkv, wq, wk, wv, ta, tb)


def _attn_kernel(q_ref, k_ref, v_ref, o_ref, *, group, tk, unroll):
    tq = q_ref.shape[0]
    seq = k_ref.shape[0]
    if group == 1:
        q = q_ref[...]
    else:
        q = jnp.concatenate(
            [q_ref[:, g * LANES:(g + 1) * LANES] for g in range(group)], axis=0)
    m_rows = group * tq

    def scores(j):
        k = k_ref[pl.ds(pl.multiple_of(j * tk, tk), tk), :]
        return lax.dot_general(q, k, (((1,), (1,)), ((), ())), preferred_element_type=F32)

    def update(j, m, acc, s):
        v = v_ref[pl.ds(pl.multiple_of(j * tk, tk), tk), :]
        m_new = jnp.maximum(m, jnp.max(s, axis=-1, keepdims=True))
        alpha = jnp.exp2(m - m_new)
        p = jnp.exp2((s - m_new).astype(BF16))
        return m_new, alpha * acc + _dot(p, v)

    def body(j, carry):
        m, acc = carry
        return update(j, m, acc, scores(j))

    n_blk = seq // tk
    init = (jnp.full((m_rows, 1), -jnp.inf, F32), jnp.zeros((m_rows, LANES), F32))
    _, acc = lax.fori_loop(0, n_blk, body, init, unroll=unroll if n_blk % unroll == 0 else 1)
    o = (acc / acc[:, ONES_LANE:ONES_LANE + 1]).astype(o_ref.dtype)
    for g in range(group):
        o_ref[:, g * LANES:(g + 1) * LANES] = o[g * tq:(g + 1) * tq, :]


def _attention(q, k, v, *, group, tq, tk, unroll):
    bsz, seq, qw = q.shape
    n_kv = k.shape[-1] // LANES
    return pl.pallas_call(
        functools.partial(_attn_kernel, group=group, tk=tk, unroll=unroll),
        out_shape=jax.ShapeDtypeStruct((bsz, seq, qw), BF16),
        grid=(bsz, n_kv, seq // tq),
        in_specs=[
            pl.BlockSpec((None, tq, group * LANES), lambda b, h, i: (b, i, h)),
            pl.BlockSpec((None, seq, LANES), lambda b, h, i: (b, 0, h)),
            pl.BlockSpec((None, seq, LANES), lambda b, h, i: (b, 0, h)),
        ],
        out_specs=pl.BlockSpec((None, tq, group * LANES), lambda b, h, i: (b, i, h)),
        compiler_params=_cparams(("parallel", "parallel", "parallel")),
        name="flash_attn_g%d" % group,
    )(q, k, v)


def _first_argmax(vals):
    best, idx = vals[0], jnp.zeros_like(vals[0], dtype=jnp.int32)
    for i in range(1, len(vals)):
        gt = vals[i] > best
        best = jnp.where(gt, vals[i], best)
        idx = jnp.where(gt, i, idx)
    return best, idx


def _pick(vals, idx):
    out = vals[0]
    for i in range(1, len(vals)):
        out = jnp.where(idx == i, vals[i], out)
    return out


def _outproj_kernel(oa_ref, ob_ref, ga_ref, gb_ref, x_ref, mod_ref, n2_ref,
                    wpa_ref, wpb_ref, wo_ref, wr_ref, rb_ref,
                    x1_ref, h2_ref, bkt_ref, rank_ref, cnt_ref, base_ref, gwt_ref):
    tm = x_ref.shape[0]

    @pl.when((pl.program_id(0) == 0) & (pl.program_id(1) == 0))
    def _():
        base_ref[...] = jnp.zeros_like(base_ref)

    ya = _dot(oa_ref[...], wpa_ref[...])
    yb = _dot(ob_ref[...], wpb_ref[...])
    merged = ga_ref[...].astype(F32) * ya + gb_ref[...].astype(F32) * yb
    att = _dot(merged.astype(BF16), wo_ref[...])
    g1 = mod_ref[2:3, :]
    sh2 = mod_ref[3:4, :]
    sc2 = mod_ref[4:5, :]
    x1 = x_ref[...] + g1 * att
    x1_ref[...] = x1
    h2 = (_rms(x1, D_MODEL) * n2_ref[...]) * (1.0 + sc2) + sh2
    h2_ref[:, 0:D_MODEL] = h2

    logits = _dot3(h2, wr_ref[...])
    lt = logits.T[0:N_EXPERTS, :]
    scores = jax.nn.sigmoid(lt)
    biased = scores + rb_ref[...]
    brow = [biased[e:e + 1, :] for e in range(N_EXPERTS)]
    srow = [scores[e:e + 1, :] for e in range(N_EXPERTS)]
    grp = []
    for g in range(N_GROUPS):
        a, b, c, d = brow[4 * g:4 * g + 4]
        hi1, lo1 = jnp.maximum(a, b), jnp.minimum(a, b)
        hi2, lo2 = jnp.maximum(c, d), jnp.minimum(c, d)
        grp.append(jnp.maximum(hi1, hi2) + jnp.maximum(jnp.minimum(hi1, hi2),
                                                       jnp.maximum(lo1, lo2)))
    _, gsel = _first_argmax(grp)
    bsel = [_pick([brow[4 * g + i] for g in range(N_GROUPS)], gsel)
            for i in range(EXPERTS_PER_GROUP)]
    ssel = [_pick([srow[4 * g + i] for g in range(N_GROUPS)], gsel)
            for i in range(EXPERTS_PER_GROUP)]
    _, i0 = _first_argmax(bsel)
    _, i1 = _first_argmax([jnp.where(i0 == i, -jnp.inf, bsel[i])
                           for i in range(EXPERTS_PER_GROUP)])
    w0 = _pick(ssel, i0)
    w1 = _pick(ssel, i1)
    wsum = w0 + w1
    first_lo = i0 < i1
    lo = jnp.where(first_lo, i0, i1)
    hi = jnp.where(first_lo, i1, i0)
    w_lo = jnp.where(first_lo, w0, w1) / wsum
    w_hi = jnp.where(first_lo, w1, w0) / wsum
    pair = jnp.where(lo == 0, 0, jnp.where(lo == 1, 3, 5)) + (hi - lo - 1)
    bucket = gsel * N_PAIRS + pair
    bkt_ref[...] = bucket

    gwt_ref[...] = jnp.zeros_like(gwt_ref)
    gwt_ref[0:1, :] = w_lo
    gwt_ref[1:2, :] = w_hi
    h2_ref[:, D_MODEL:] = gwt_ref[...].T

    onehot = lax.broadcasted_iota(jnp.int32, (N_BUCKETS, tm), 0) == bucket
    cnt = jnp.where(onehot, 1.0, 0.0)
    r_i = lax.broadcasted_iota(jnp.int32, (tm, tm), 0)
    c_i = lax.broadcasted_iota(jnp.int32, (tm, tm), 1)
    upper = jnp.where(r_i < c_i, 1.0, 0.0).astype(BF16)
    pref = _dot(cnt.astype(BF16), upper) + base_ref[:, 0:1]
    rank_ref[...] = jnp.sum(jnp.where(onehot, pref, 0.0), axis=0, keepdims=True).astype(jnp.int32)
    base_ref[...] = base_ref[...] + jnp.sum(cnt, axis=1, keepdims=True)
    cnt_ref[...] = base_ref[...]


def _outproj(oa, ob, ga, gb, x, mod_l, n2, wpa, wpb, wo, wr, rb, tm):
    bsz, seq, d = x.shape
    nt = seq // tm
    row = lambda w: pl.BlockSpec((None, tm, w), lambda b, i: (b, i, 0))
    full = lambda a: pl.BlockSpec(a.shape, lambda b, i: (0,) * a.ndim)
    tok = pl.BlockSpec((1, tm), lambda b, i: (0, b * nt + i))
    t = bsz * seq
    return pl.pallas_call(
        _outproj_kernel,
        out_shape=[
            jax.ShapeDtypeStruct((bsz, seq, d), F32),
            jax.ShapeDtypeStruct((bsz, seq, XS_COLS), F32),
            jax.ShapeDtypeStruct((1, t), jnp.int32),
            jax.ShapeDtypeStruct((1, t), jnp.int32),
            jax.ShapeDtypeStruct((N_BUCKETS, LANES), F32),
        ],
        grid=(bsz, nt),
        in_specs=[
            row(d), row(d), row(d), row(d), row(d),
            pl.BlockSpec((None, N_MOD, d), lambda b, i: (b, 0, 0)),
            full(n2), full(wpa), full(wpb), full(wo), full(wr), full(rb),
        ],
        out_specs=[row(d), row(XS_COLS), tok, tok,
                   pl.BlockSpec((N_BUCKETS, LANES), lambda b, i: (0, 0))],
        scratch_shapes=[pltpu.VMEM((N_BUCKETS, LANES), F32), pltpu.VMEM((LANES, tm), F32)],
        compiler_params=_cparams(("arbitrary", "arbitrary")),
        name="outproj_router",
    )(oa, ob, ga, gb, x, mod_l, n2, wpa, wpb, wo, wr, rb)


def _dispatch_kernel(dest_ref, h_ref, xs_ref, sem):
    tm = h_ref.shape[0]

    def issue(t, c):
        pltpu.make_async_copy(h_ref.at[pl.ds(t, 1)], xs_ref.at[pl.ds(dest_ref[0, t], 1)],
                              sem).start()
        return c

    lax.fori_loop(0, tm, issue, 0, unroll=DMA_UNROLL)
    pltpu.make_async_copy(h_ref, xs_ref.at[pl.ds(0, tm)], sem).wait()


def _dispatch(h2, dest, n_rows, tm):
    t, d = h2.shape
    nt = t // tm
    return pl.pallas_call(
        _dispatch_kernel,
        out_shape=jax.ShapeDtypeStruct((n_rows, d), F32),
        grid=(nt,),
        in_specs=[
            pl.BlockSpec((None, 1, tm), lambda i: (i, 0, 0), memory_space=pltpu.SMEM),
            pl.BlockSpec((tm, d), lambda i: (i, 0)),
        ],
        out_specs=pl.BlockSpec(memory_space=pl.ANY),
        scratch_shapes=[pltpu.SemaphoreType.DMA],
        compiler_params=_cparams(("arbitrary",)),
        name="moe_dispatch",
    )(dest.reshape(nt, 1, tm), h2)


def _expert_kernel(ea_ref, eb_ref, br_ref, nv_ref, x_ref, w1a_ref, w3a_ref, w2a_ref,
                   w1b_ref, w3b_ref, w2b_ref, y_ref):
    b = pl.program_id(0)

    @pl.when(b < nv_ref[0])
    def _():
        bm = x_ref.shape[0]
        live = lax.broadcasted_iota(jnp.int32, (bm, 1), 0) < br_ref[b]
        x = jnp.where(live, x_ref[:, 0:D_MODEL], 0.0).astype(BF16)
        gates = jnp.where(live, x_ref[:, D_MODEL:], 0.0)

        def hidden(w1_ref, w3_ref, gate):
            a = _dot(x, w1_ref[...])
            g = _dot(x, w3_ref[...])
            return ((a * jax.nn.sigmoid(a)) * g * gate).astype(BF16)

        h_lo = hidden(w1a_ref, w3a_ref, gates[:, 0:1])
        h_hi = hidden(w1b_ref, w3b_ref, gates[:, 1:2])
        y_ref[...] = _dot(h_lo, w2a_ref[...]) + _dot(h_hi, w2b_ref[...])


def _experts(xs, w1, w3, w2, layer, blk_ea, blk_eb, blk_rows, n_valid, bm):
    n_rows = xs.shape[0]
    nb = n_rows // bm
    d, f = w1.shape[-2:]
    last = lambda b, nv: jnp.minimum(b, nv[0] - 1)
    xmap = lambda b, ea, eb, br, nv: (last(b, nv), 0)
    amap = lambda b, ea, eb, br, nv: (layer, ea[last(b, nv)], 0, 0)
    bmap = lambda b, ea, eb, br, nv: (layer, eb[last(b, nv)], 0, 0)
    return pl.pallas_call(
        _expert_kernel,
        out_shape=jax.ShapeDtypeStruct((n_rows, d), F32),
        grid_spec=pltpu.PrefetchScalarGridSpec(
            num_scalar_prefetch=4,
            grid=(nb,),
            in_specs=[
                pl.BlockSpec((bm, XS_COLS), xmap),
                pl.BlockSpec((None, None, d, f), amap),
                pl.BlockSpec((None, None, d, f), amap),
                pl.BlockSpec((None, None, f, d), amap),
                pl.BlockSpec((None, None, d, f), bmap),
                pl.BlockSpec((None, None, d, f), bmap),
                pl.BlockSpec((None, None, f, d), bmap),
            ],
            out_specs=pl.BlockSpec((bm, d), xmap),
        ),
        compiler_params=_cparams(("arbitrary",)),
        name="moe_experts",
    )(blk_ea, blk_eb, blk_rows, n_valid, xs, w1, w3, w2, w1, w3, w2)


def _combine_kernel(dest_ref, x1_ref, mod_ref, nf_ref, ys_ref, o_ref, buf, sem, *, final):
    tm = x1_ref.shape[0]

    def issue(t, c):
        pltpu.make_async_copy(ys_ref.at[pl.ds(dest_ref[0, t], 1)], buf.at[pl.ds(t, 1)],
                              sem).start()
        return c

    lax.fori_loop(0, tm, issue, 0, unroll=DMA_UNROLL)
    pltpu.make_async_copy(ys_ref.at[pl.ds(0, tm)], buf, sem).wait()
    x2 = x1_ref[...] + mod_ref[5:6, :] * buf[...]
    if final:
        x2 = _rms(x2, D_MODEL) * nf_ref[...]
    o_ref[...] = x2


def _combine(ys, dest, x1, mod_l, nf, tm, final):
    bsz, seq, d = x1.shape
    nt = seq // tm
    return pl.pallas_call(
        functools.partial(_combine_kernel, final=final),
        out_shape=jax.ShapeDtypeStruct((bsz, seq, d), F32),
        grid=(bsz, nt),
        in_specs=[
            pl.BlockSpec((None, 1, tm), lambda b, i: (b * nt + i, 0, 0),
                         memory_space=pltpu.SMEM),
            pl.BlockSpec((None, tm, d), lambda b, i: (b, i, 0)),
            pl.BlockSpec((None, N_MOD, d), lambda b, i: (b, 0, 0)),
            pl.BlockSpec((1, d), lambda b, i: (0, 0)),
            pl.BlockSpec(memory_space=pl.ANY),
        ],
        out_specs=pl.BlockSpec((None, tm, d), lambda b, i: (b, i, 0)),
        scratch_shapes=[pltpu.VMEM((tm, d), F32), pltpu.SemaphoreType.DMA],
        compiler_params=_cparams(("arbitrary", "arbitrary")),
        name="moe_combine",
    )(dest.reshape(bsz * nt, 1, tm), x1, mod_l, nf, ys)


def _rope_tables(seq):
    rows = seq // GRID_W
    row = jnp.broadcast_to(jnp.arange(rows, dtype=F32)[:, None], (rows, GRID_W)).reshape(seq)
    col = jnp.broadcast_to(jnp.arange(GRID_W, dtype=F32)[None, :], (rows, GRID_W)).reshape(seq)

    def cs(rot_dim):
        n = rot_dim // 4
        inv = 1.0 / (ROPE_THETA ** (jnp.arange(n, dtype=F32) / n))
        ang = jnp.concatenate([row[:, None] * inv, col[:, None] * inv], axis=-1)
        return jnp.cos(ang), jnp.sin(ang)

    def place(pieces):
        out = jnp.zeros((seq, LANES), F32)
        for off, val in pieces:
            out = out.at[:, off:off + val.shape[1]].set(val)
        return out

    ca, sa = cs(HEAD_DIM_A)
    ha = HEAD_DIM_A // 2
    scale_a = HEAD_DIM_A ** -0.5 * LOG2E
    ta = jnp.stack([place([(0, ca), (ha, ca)]), place([(0, -sa)]), place([(ha, sa)])]) * scale_a
    cb, sb = cs(QK_ROPE)
    hb = QK_ROPE // 2
    scale_b = (QK_NOPE + QK_ROPE) ** -0.5 * LOG2E
    ones = jnp.ones((seq, QK_NOPE), F32)
    cos_q = place([(0, ones), (QK_NOPE, cb), (QK_NOPE + hb, cb)]) * scale_b
    sin_lo = place([(QK_NOPE, -sb)])
    sin_hi = place([(QK_NOPE + hb, sb)])
    cos_k = place([(QK_NOPE, cb), (QK_NOPE + hb, cb)])
    tb = jnp.stack([cos_q, sin_lo * scale_b, sin_hi * scale_b, cos_k, sin_lo, sin_hi])
    return ta, tb


def _pad_heads(w, n_heads, width, offset=0):
    k = w.shape[0]
    w = w.reshape(k, n_heads, width)
    out = jnp.zeros((k, n_heads, LANES), w.dtype).at[:, :, offset:offset + width].set(w)
    return out.reshape(k, n_heads * LANES)


def _prep_layer(w_in, w_q_b, w_kv_b, w_proj_a, w_proj_b):
    o = 0
    parts = []
    sizes = (N_HEADS_A * HEAD_DIM_A, N_KV_A * HEAD_DIM_A, N_KV_A * HEAD_DIM_A,
             Q_LORA, KV_LORA, QK_ROPE, D_MODEL, D_MODEL)
    segs = []
    for s in sizes:
        segs.append(w_in[:, o:o + s])
        o += s
    parts = [
        _pad_heads(segs[0], N_HEADS_A, HEAD_DIM_A),
        _pad_heads(segs[1], N_KV_A, HEAD_DIM_A),
        _pad_heads(segs[2], N_KV_A, HEAD_DIM_A),
        segs[3], segs[4],
        _pad_heads(segs[5], 1, QK_ROPE, offset=QK_NOPE),
        segs[6], segs[7],
    ]
    win = jnp.concatenate(parts, axis=1).astype(BF16)
    wq = _pad_heads(w_q_b, N_HEADS_B, QK_NOPE + QK_ROPE).astype(BF16)
    wkv = w_kv_b.reshape(KV_LORA, N_HEADS_B, QK_NOPE + V_DIM_B)
    wk = _pad_heads(wkv[:, :, :QK_NOPE].reshape(KV_LORA, -1), N_HEADS_B, QK_NOPE).astype(BF16)
    wv = _pad_heads(wkv[:, :, QK_NOPE:].reshape(KV_LORA, -1), N_HEADS_B, V_DIM_B).astype(BF16)

    def pad_rows(w, n_heads, width):
        return _pad_heads(w.T, n_heads, width).T.astype(BF16)

    wpa = pad_rows(w_proj_a, N_HEADS_A, HEAD_DIM_A)
    wpb = pad_rows(w_proj_b, N_HEADS_B, V_DIM_B)
    return win, wq, wk, wv, wpa, wpb


def _pad_lanes(g, offset=0):
    return jnp.zeros((1, LANES), F32).at[0, offset:offset + g.shape[0]].set(g)


def kernel(x, c, w_ada, b_ada, norm1, w_in, q_norm_a, k_norm_a, q_a_norm, kv_norm, w_q_b, w_kv_b, w_proj_a, w_proj_b, w_o, norm2, w_router, router_bias, w1, w3, w2, norm_f):
    bsz, seq, d = x.shape
    depth = w_in.shape[0]
    t = bsz * seq
    tm = min(256, seq)
    bm = min(256, t)
    tk = min(512, seq)

    mod = _modulation(c, w_ada, b_ada).reshape(depth, bsz, N_MOD, d)
    ta, tb = _rope_tables(seq)
    wr = jnp.zeros((d, LANES), F32).at[:, :N_EXPERTS].set(w_router)
    rb = router_bias.reshape(N_EXPERTS, 1)
    n_rows = ((t + bm - 1) // bm) * bm + N_BUCKETS * bm
    nb = n_rows // bm
    w1b, w3b, w2b = w1.astype(BF16), w3.astype(BF16), w2.astype(BF16)
    nf = norm_f.reshape(1, d)
    bucket_ids = jnp.arange(N_BUCKETS, dtype=jnp.int32)
    bucket_ea = (bucket_ids // N_PAIRS) * EXPERTS_PER_GROUP + jnp.array(PAIR_LO, jnp.int32)[bucket_ids % N_PAIRS]
    bucket_eb = (bucket_ids // N_PAIRS) * EXPERTS_PER_GROUP + jnp.array(PAIR_HI, jnp.int32)[bucket_ids % N_PAIRS]

    for l in range(depth):
        win, wq, wk, wv, wpa, wpb = _prep_layer(w_in[l], w_q_b[l], w_kv_b[l],
                                                w_proj_a[l], w_proj_b[l])
        qa, ka, va, qb, kb, vb, ga, gb = _inproj(
            x, mod[l], norm1[l].reshape(1, d), win,
            _pad_lanes(q_norm_a[l]), _pad_lanes(k_norm_a[l]),
            q_a_norm[l].reshape(1, Q_LORA), kv_norm[l].reshape(1, KV_LORA),
            wq, wk, wv, ta, tb, tm)
        oa = _attention(qa, ka, va, group=GROUP_A, tq=min(128, seq), tk=tk, unroll=8)
        ob = _attention(qb, kb, vb, group=1, tq=min(512, seq), tk=tk, unroll=8)
        x1, h2, bucket, rank, cnt = _outproj(
            oa, ob, ga, gb, x, mod[l], norm2[l].reshape(1, d), wpa, wpb,
            w_o[l].astype(BF16), wr, rb, tm)

        counts = cnt[:, 0].astype(jnp.int32)
        padded = ((counts + bm - 1) // bm) * bm
        pends = jnp.cumsum(padded)
        pstarts = pends - padded
        dest = rank
        for e in range(N_BUCKETS):
            dest = dest + jnp.where(bucket == e, pstarts[e], 0)
        blk_start = jnp.arange(nb, dtype=jnp.int32) * bm
        blk_b = jnp.minimum(jnp.sum(blk_start[:, None] >= pends[None, :], axis=1),
                            N_BUCKETS - 1).astype(jnp.int32)
        blk_rows = jnp.clip(pstarts[blk_b] + counts[blk_b] - blk_start, 0, bm).astype(jnp.int32)
        n_valid = jnp.maximum(pends[-1:] // bm, 1).astype(jnp.int32)

        xs = _dispatch(h2.reshape(t, XS_COLS), dest, n_rows, tm)
        ys = _experts(xs, w1b, w3b, w2b, l, bucket_ea[blk_b], bucket_eb[blk_b], blk_rows,
                      n_valid, bm)
        x = _combine(ys, dest, x1, mod[l], nf, tm, final=(l == depth - 1))
    return x
```

```python
import functools

import jax
import jax.numpy as jnp
from jax import lax
from jax.experimental import pallas as pl
from jax.experimental.pallas import tpu as pltpu

F32 = jnp.float32
BF16 = jnp.bfloat16

D_MODEL = 1024
GRID_W = 64
ROPE_THETA = 10000.0
EPS = 1e-6
N_HEADS_A = 8
N_KV_A = 2
GROUP_A = N_HEADS_A // N_KV_A
HEAD_DIM_A = 64
N_HEADS_B = 8
Q_LORA = 384
KV_LORA = 256
QK_NOPE = 64
QK_ROPE = 32
V_DIM_B = 64
N_MOD = 6
N_EXPERTS = 16
N_GROUPS = 4
EXPERTS_PER_GROUP = N_EXPERTS // N_GROUPS
D_EXPERT = 512
N_PAIRS = EXPERTS_PER_GROUP * (EXPERTS_PER_GROUP - 1) // 2
N_BUCKETS = N_GROUPS * N_PAIRS
PAIR_LO = (0, 0, 0, 1, 1, 2)
PAIR_HI = (1, 2, 3, 2, 3, 3)

LANES = 128
ONES_LANE = 64
LOG2E = 1.4426950408889634
XS_COLS = D_MODEL + LANES
DMA_UNROLL = 8
VMEM_LIMIT = 56 * 1024 * 1024

_C_QA = 0
_C_KA = _C_QA + N_HEADS_A * LANES
_C_VA = _C_KA + N_KV_A * LANES
_C_QL = _C_VA + N_KV_A * LANES
_C_KVL = _C_QL + Q_LORA
_C_KR = _C_KVL + KV_LORA
_C_GA = _C_KR + LANES
_C_GB = _C_GA + D_MODEL
_C_END = _C_GB + D_MODEL


def _cparams(sem):
    return pltpu.CompilerParams(dimension_semantics=sem, vmem_limit_bytes=VMEM_LIMIT)


def _resident(a):
    return pl.BlockSpec(a.shape, lambda *_: (0,) * a.ndim, pipeline_mode=pl.Buffered(1))


def _split(a):
    hi = a.astype(BF16)
    lo = (a - hi.astype(F32)).astype(BF16)
    return hi, lo


def _dot(a, b):
    return jnp.dot(a, b, preferred_element_type=F32)


def _dot3(a, b):
    ah, al = _split(a)
    bh, bl = _split(b)
    return _dot(ah, bh) + (_dot(ah, bl) + _dot(al, bh))


def _mod_kernel(c_ref, w_ref, b_ref, o_ref):
    c = c_ref[...]
    cond = c * jax.nn.sigmoid(c)
    o_ref[...] = _dot3(cond, w_ref[...]) + b_ref[...]


def _modulation(c, w_ada, b_ada):
    depth, d, n = w_ada.shape
    bsz = c.shape[0]
    cb = 1536
    return pl.pallas_call(
        _mod_kernel,
        out_shape=jax.ShapeDtypeStruct((depth, bsz, n), F32),
        grid=(depth, n // cb),
        in_specs=[
            pl.BlockSpec((bsz, d), lambda l, j: (0, 0)),
            pl.BlockSpec((None, d, cb), lambda l, j: (l, 0, j)),
            pl.BlockSpec((None, 1, cb), lambda l, j: (l, 0, j)),
        ],
        out_specs=pl.BlockSpec((None, bsz, cb), lambda l, j: (l, 0, j)),
        compiler_params=_cparams(("parallel", "parallel")),
        name="adaln_mod",
    )(c, w_ada, b_ada.reshape(depth, 1, n))


def _rms(v, n):
    return v * lax.rsqrt(jnp.sum(v * v, axis=-1, keepdims=True) * (1.0 / n) + EPS)


def _rope(v, cos, sin_lo, sin_hi, shift):
    return (v * cos + pltpu.roll(v, LANES - shift, 1) * sin_lo
            + pltpu.roll(v, shift, 1) * sin_hi)


def _ones_lane(width):
    lane = lax.broadcasted_iota(jnp.int32, (1, width), 1)
    return jnp.where(lane % LANES == ONES_LANE, 1.0, 0.0)


def _inproj_kernel(x_ref, mod_ref, n1_ref, win_ref, gqa_ref, gka_ref, gql_ref, gkv_ref,
                   wq_ref, wk_ref, wv_ref, ta_ref, tb_ref,
                   qa_ref, ka_ref, va_ref, qb_ref, kb_ref, vb_ref, ga_ref, gb_ref):
    x = x_ref[...]
    sh1 = mod_ref[0:1, :]
    sc1 = mod_ref[1:2, :]
    h = (_rms(x, D_MODEL) * n1_ref[...]) * (1.0 + sc1) + sh1
    hb = h.astype(BF16)

    cos_a, sin_a_lo, sin_a_hi = ta_ref[0], ta_ref[1], ta_ref[2]
    cos_b, sin_b_lo, sin_b_hi = tb_ref[0], tb_ref[1], tb_ref[2]
    cos_k, sin_k_lo, sin_k_hi = tb_ref[3], tb_ref[4], tb_ref[5]

    gqa = gqa_ref[...]
    z = _dot(hb, win_ref[:, _C_QA:_C_KA])
    for hh in range(N_HEADS_A):
        zn = _rms(z[:, hh * LANES:(hh + 1) * LANES], HEAD_DIM_A) * gqa
        qa_ref[:, hh * LANES:(hh + 1) * LANES] = _rope(
            zn, cos_a, sin_a_lo, sin_a_hi, HEAD_DIM_A // 2).astype(BF16)
    gka = gka_ref[...]
    z = _dot(hb, win_ref[:, _C_KA:_C_QL])
    for hh in range(N_KV_A):
        zn = _rms(z[:, hh * LANES:(hh + 1) * LANES], HEAD_DIM_A) * gka
        ka_ref[:, hh * LANES:(hh + 1) * LANES] = _rope(
            zn, cos_a, sin_a_lo, sin_a_hi, HEAD_DIM_A // 2).astype(BF16)
    va_ref[...] = (z[:, N_KV_A * LANES:] + _ones_lane(N_KV_A * LANES)).astype(BF16)

    z = _dot(hb, win_ref[:, _C_QL:_C_GA])
    qlb = (_rms(z[:, 0:Q_LORA], Q_LORA) * gql_ref[...]).astype(BF16)
    kvb = (_rms(z[:, Q_LORA:Q_LORA + KV_LORA], KV_LORA) * gkv_ref[...]).astype(BF16)
    kr = _rope(z[:, Q_LORA + KV_LORA:], cos_k, sin_k_lo, sin_k_hi, QK_ROPE // 2)
    zq = _dot(qlb, wq_ref[...])
    for hh in range(N_HEADS_B):
        qb_ref[:, hh * LANES:(hh + 1) * LANES] = _rope(
            zq[:, hh * LANES:(hh + 1) * LANES], cos_b, sin_b_lo, sin_b_hi,
            QK_ROPE // 2).astype(BF16)

    zk = _dot(kvb, wk_ref[...])
    for hh in range(N_HEADS_B):
        kb_ref[:, hh * LANES:(hh + 1) * LANES] = (zk[:, hh * LANES:(hh + 1) * LANES] + kr).astype(BF16)
    vb_ref[...] = (_dot(kvb, wv_ref[...]) + _ones_lane(N_HEADS_B * LANES)).astype(BF16)

    ga_ref[...] = jax.nn.sigmoid(_dot(hb, win_ref[:, _C_GA:_C_GB])).astype(BF16)
    gb_ref[...] = jax.nn.sigmoid(_dot(hb, win_ref[:, _C_GB:_C_END])).astype(BF16)


def _inproj(x, mod_l, n1, win, gqa, gka, gql, gkv, wq, wk, wv, ta, tb, tm):
    bsz, seq, d = x.shape
    nt = seq // tm
    row = lambda w: pl.BlockSpec((None, tm, w), lambda b, i: (b, i, 0))
    full = _resident
    widths = (N_HEADS_A * LANES, N_KV_A * LANES, N_KV_A * LANES,
              N_HEADS_B * LANES, N_HEADS_B * LANES, N_HEADS_B * LANES, D_MODEL, D_MODEL)
    return pl.pallas_call(
        _inproj_kernel,
        out_shape=[jax.ShapeDtypeStruct((bsz, seq, w), BF16) for w in widths],
        grid=(bsz, nt),
        in_specs=[
            row(d),
            pl.BlockSpec((None, N_MOD, d), lambda b, i: (b, 0, 0)),
            full(n1), full(win), full(gqa), full(gka), full(gql), full(gkv),
            full(wq), full(wk), full(wv),
            pl.BlockSpec((3, tm, LANES), lambda b, i: (0, i, 0)),
            pl.BlockSpec((6, tm, LANES), lambda b, i: (0, i, 0)),
        ],
        out_specs=[row(w) for w in widths],
        compiler_params=_cparams(("parallel", "parallel")),
        name="inproj",
    )(x, mod_l, n1, win, gqa, gka, gql, gkv, wq, wk, wv, ta, tb)


def _attn_kernel(q_ref, k_ref, v_ref, o_ref, *, group, tk, unroll, chains):
    tq = q_ref.shape[0]
    seq = k_ref.shape[0]
    if group == 1:
        q = q_ref[...]
    else:
        q = jnp.concatenate(
            [q_ref[:, g * LANES:(g + 1) * LANES] for g in range(group)], axis=0)
    m_rows = group * tq
    rows_c = m_rows // chains
    qs = [q[c * rows_c:(c + 1) * rows_c, :] for c in range(chains)]

    def body(j, carry):
        start = pl.multiple_of(j * tk, tk)
        k = k_ref[pl.ds(start, tk), :]
        v = v_ref[pl.ds(start, tk), :]
        ss = [lax.dot_general(qc, k, (((1,), (1,)), ((), ())), preferred_element_type=F32)
              for qc in qs]
        out = []
        for (m, acc), s in zip(carry, ss):
            m_new = jnp.maximum(m, jnp.max(s, axis=-1, keepdims=True))
            alpha = jnp.exp2(m - m_new)
            p = jnp.exp2((s - m_new).astype(BF16))
            out.append((m_new, alpha * acc + _dot(p, v)))
        return tuple(out)

    n_blk = seq // tk
    init = tuple((jnp.full((rows_c, 1), -jnp.inf, F32), jnp.zeros((rows_c, LANES), F32))
                 for _ in range(chains))
    fin = lax.fori_loop(0, n_blk, body, init, unroll=unroll if n_blk % unroll == 0 else 1)
    acc = jnp.concatenate([a for _, a in fin], axis=0)
    o = (acc / acc[:, ONES_LANE:ONES_LANE + 1]).astype(o_ref.dtype)
    for g in range(group):
        o_ref[:, g * LANES:(g + 1) * LANES] = o[g * tq:(g + 1) * tq, :]


def _attention(q, k, v, *, group, tq, tk, unroll, chains):
    bsz, seq, qw = q.shape
    n_kv = k.shape[-1] // LANES
    return pl.pallas_call(
        functools.partial(_attn_kernel, group=group, tk=tk, unroll=unroll, chains=chains),
        out_shape=jax.ShapeDtypeStruct((bsz, seq, qw), BF16),
        grid=(bsz, n_kv, seq // tq),
        in_specs=[
            pl.BlockSpec((None, tq, group * LANES), lambda b, h, i: (b, i, h)),
            pl.BlockSpec((None, seq, LANES), lambda b, h, i: (b, 0, h)),
            pl.BlockSpec((None, seq, LANES), lambda b, h, i: (b, 0, h)),
        ],
        out_specs=pl.BlockSpec((None, tq, group * LANES), lambda b, h, i: (b, i, h)),
        compiler_params=_cparams(("parallel", "parallel", "parallel")),
        name="flash_attn_g%d" % group,
    )(q, k, v)


def _first_argmax(vals):
    best, idx = vals[0], jnp.zeros_like(vals[0], dtype=jnp.int32)
    for i in range(1, len(vals)):
        gt = vals[i] > best
        best = jnp.where(gt, vals[i], best)
        idx = jnp.where(gt, i, idx)
    return best, idx


def _pick(vals, idx):
    out = vals[0]
    for i in range(1, len(vals)):
        out = jnp.where(idx == i, vals[i], out)
    return out


def _outproj_kernel(oa_ref, ob_ref, ga_ref, gb_ref, x_ref, mod_ref, n2_ref,
                    wpa_ref, wpb_ref, wo_ref, wr_ref, rb_ref,
                    x1_ref, h2_ref, bkt_ref, rank_ref, cnt_ref, base_ref, gwt_ref):
    tm = x_ref.shape[0]

    @pl.when((pl.program_id(0) == 0) & (pl.program_id(1) == 0))
    def _():
        base_ref[...] = jnp.zeros_like(base_ref)

    ya = _dot(oa_ref[...], wpa_ref[...])
    yb = _dot(ob_ref[...], wpb_ref[...])
    merged = ga_ref[...].astype(F32) * ya + gb_ref[...].astype(F32) * yb
    att = _dot(merged.astype(BF16), wo_ref[...])
    g1 = mod_ref[2:3, :]
    sh2 = mod_ref[3:4, :]
    sc2 = mod_ref[4:5, :]
    x1 = x_ref[...] + g1 * att
    x1_ref[...] = x1
    h2 = (_rms(x1, D_MODEL) * n2_ref[...]) * (1.0 + sc2) + sh2
    h2_ref[:, 0:D_MODEL] = h2

    logits = _dot3(h2, wr_ref[...])
    lt = logits.T[0:N_EXPERTS, :]
    scores = jax.nn.sigmoid(lt)
    biased = scores + rb_ref[...]
    brow = [biased[e:e + 1, :] for e in range(N_EXPERTS)]
    srow = [scores[e:e + 1, :] for e in range(N_EXPERTS)]
    grp = []
    for g in range(N_GROUPS):
        a, b, c, d = brow[4 * g:4 * g + 4]
        hi1, lo1 = jnp.maximum(a, b), jnp.minimum(a, b)
        hi2, lo2 = jnp.maximum(c, d), jnp.minimum(c, d)
        grp.append(jnp.maximum(hi1, hi2) + jnp.maximum(jnp.minimum(hi1, hi2),
                                                       jnp.maximum(lo1, lo2)))
    _, gsel = _first_argmax(grp)
    bsel = [_pick([brow[4 * g + i] for g in range(N_GROUPS)], gsel)
            for i in range(EXPERTS_PER_GROUP)]
    ssel = [_pick([srow[4 * g + i] for g in range(N_GROUPS)], gsel)
            for i in range(EXPERTS_PER_GROUP)]
    _, i0 = _first_argmax(bsel)
    _, i1 = _first_argmax([jnp.where(i0 == i, -jnp.inf, bsel[i])
                           for i in range(EXPERTS_PER_GROUP)])
    w0 = _pick(ssel, i0)
    w1 = _pick(ssel, i1)
    wsum = w0 + w1
    first_lo = i0 < i1
    lo = jnp.where(first_lo, i0, i1)
    hi = jnp.where(first_lo, i1, i0)
    w_lo = jnp.where(first_lo, w0, w1) / wsum
    w_hi = jnp.where(first_lo, w1, w0) / wsum
    pair = jnp.where(lo == 0, 0, jnp.where(lo == 1, 3, 5)) + (hi - lo - 1)
    bucket = gsel * N_PAIRS + pair
    bkt_ref[...] = bucket

    gwt_ref[...] = jnp.zeros_like(gwt_ref)
    gwt_ref[0:1, :] = w_lo
    gwt_ref[1:2, :] = w_hi
    h2_ref[:, D_MODEL:] = gwt_ref[...].T

    onehot = lax.broadcasted_iota(jnp.int32, (N_BUCKETS, tm), 0) == bucket
    cnt = jnp.where(onehot, 1.0, 0.0)
    r_i = lax.broadcasted_iota(jnp.int32, (tm, tm), 0)
    c_i = lax.broadcasted_iota(jnp.int32, (tm, tm), 1)
    upper = jnp.where(r_i < c_i, 1.0, 0.0).astype(BF16)
    pref = _dot(cnt.astype(BF16), upper) + base_ref[:, 0:1]
    rank_ref[...] = jnp.sum(jnp.where(onehot, pref, 0.0), axis=0, keepdims=True).astype(jnp.int32)
    base_ref[...] = base_ref[...] + jnp.sum(cnt, axis=1, keepdims=True)
    cnt_ref[...] = base_ref[...]


def _outproj(oa, ob, ga, gb, x, mod_l, n2, wpa, wpb, wo, wr, rb, tm):
    bsz, seq, d = x.shape
    nt = seq // tm
    row = lambda w: pl.BlockSpec((None, tm, w), lambda b, i: (b, i, 0))
    full = _resident
    tok = pl.BlockSpec((1, tm), lambda b, i: (0, b * nt + i))
    t = bsz * seq
    return pl.pallas_call(
        _outproj_kernel,
        out_shape=[
            jax.ShapeDtypeStruct((bsz, seq, d), F32),
            jax.ShapeDtypeStruct((bsz, seq, XS_COLS), F32),
            jax.ShapeDtypeStruct((1, t), jnp.int32),
            jax.ShapeDtypeStruct((1, t), jnp.int32),
            jax.ShapeDtypeStruct((N_BUCKETS, LANES), F32),
        ],
        grid=(bsz, nt),
        in_specs=[
            row(d), row(d), row(d), row(d), row(d),
            pl.BlockSpec((None, N_MOD, d), lambda b, i: (b, 0, 0)),
            full(n2), full(wpa), full(wpb), full(wo), full(wr), full(rb),
        ],
        out_specs=[row(d), row(XS_COLS), tok, tok,
                   pl.BlockSpec((N_BUCKETS, LANES), lambda b, i: (0, 0))],
        scratch_shapes=[pltpu.VMEM((N_BUCKETS, LANES), F32), pltpu.VMEM((LANES, tm), F32)],
        compiler_params=_cparams(("arbitrary", "arbitrary")),
        name="outproj_router",
    )(oa, ob, ga, gb, x, mod_l, n2, wpa, wpb, wo, wr, rb)


def _dispatch_kernel(dest_ref, h_ref, xs_ref, sem):
    tm = h_ref.shape[0]

    def issue(t, c):
        pltpu.make_async_copy(h_ref.at[pl.ds(t, 1)], xs_ref.at[pl.ds(dest_ref[0, t], 1)],
                              sem).start()
        return c

    lax.fori_loop(0, tm, issue, 0, unroll=DMA_UNROLL)
    pltpu.make_async_copy(h_ref, xs_ref.at[pl.ds(0, tm)], sem).wait()


def _dispatch(h2, dest, n_rows, tm):
    t, d = h2.shape
    nt = t // tm
    return pl.pallas_call(
        _dispatch_kernel,
        out_shape=jax.ShapeDtypeStruct((n_rows, d), F32),
        grid=(nt,),
        in_specs=[
            pl.BlockSpec((None, 1, tm), lambda i: (i, 0, 0), memory_space=pltpu.SMEM),
            pl.BlockSpec((tm, d), lambda i: (i, 0)),
        ],
        out_specs=pl.BlockSpec(memory_space=pl.ANY),
        scratch_shapes=[pltpu.SemaphoreType.DMA],
        compiler_params=_cparams(("arbitrary",)),
        name="moe_dispatch",
    )(dest.reshape(nt, 1, tm), h2)


def _expert_kernel(ea_ref, eb_ref, br_ref, nv_ref, x_ref, w1a_ref, w3a_ref, w2a_ref,
                   w1b_ref, w3b_ref, w2b_ref, y_ref):
    b = pl.program_id(0)

    @pl.when(b < nv_ref[0])
    def _():
        bm = x_ref.shape[0]
        live = lax.broadcasted_iota(jnp.int32, (bm, 1), 0) < br_ref[b]
        x = jnp.where(live, x_ref[:, 0:D_MODEL], 0.0).astype(BF16)
        gates = jnp.where(live, x_ref[:, D_MODEL:], 0.0)

        def hidden(w1_ref, w3_ref, gate):
            a = _dot(x, w1_ref[...])
            g = _dot(x, w3_ref[...])
            return ((a * jax.nn.sigmoid(a)) * g * gate).astype(BF16)

        h_lo = hidden(w1a_ref, w3a_ref, gates[:, 0:1])
        h_hi = hidden(w1b_ref, w3b_ref, gates[:, 1:2])
        y_ref[...] = _dot(h_lo, w2a_ref[...]) + _dot(h_hi, w2b_ref[...])


def _experts(xs, w1, w3, w2, layer, blk_ea, blk_eb, blk_rows, n_valid, bm):
    n_rows = xs.shape[0]
    nb = n_rows // bm
    d, f = w1.shape[-2:]
    last = lambda b, nv: jnp.minimum(b, nv[0] - 1)
    xmap = lambda b, ea, eb, br, nv: (last(b, nv), 0)
    amap = lambda b, ea, eb, br, nv: (layer, ea[last(b, nv)], 0, 0)
    bmap = lambda b, ea, eb, br, nv: (layer, eb[last(b, nv)], 0, 0)
    return pl.pallas_call(
        _expert_kernel,
        out_shape=jax.ShapeDtypeStruct((n_rows, d), F32),
        grid_spec=pltpu.PrefetchScalarGridSpec(
            num_scalar_prefetch=4,
            grid=(nb,),
            in_specs=[
                pl.BlockSpec((bm, XS_COLS), xmap),
                pl.BlockSpec((None, None, d, f), amap),
                pl.BlockSpec((None, None, d, f), amap),
                pl.BlockSpec((None, None, f, d), amap),
                pl.BlockSpec((None, None, d, f), bmap),
                pl.BlockSpec((None, None, d, f), bmap),
                pl.BlockSpec((None, None, f, d), bmap),
            ],
            out_specs=pl.BlockSpec((bm, d), xmap),
        ),
        compiler_params=_cparams(("arbitrary",)),
        name="moe_experts",
    )(blk_ea, blk_eb, blk_rows, n_valid, xs, w1, w3, w2, w1, w3, w2)


def _combine_kernel(dest_ref, x1_ref, mod_ref, nf_ref, ys_ref, o_ref, buf, sem, *, final):
    tm = x1_ref.shape[0]

    def issue(t, c):
        pltpu.make_async_copy(ys_ref.at[pl.ds(dest_ref[0, t], 1)], buf.at[pl.ds(t, 1)],
                              sem).start()
        return c

    lax.fori_loop(0, tm, issue, 0, unroll=DMA_UNROLL)
    pltpu.make_async_copy(ys_ref.at[pl.ds(0, tm)], buf, sem).wait()
    x2 = x1_ref[...] + mod_ref[5:6, :] * buf[...]
    if final:
        x2 = _rms(x2, D_MODEL) * nf_ref[...]
    o_ref[...] = x2


def _combine(ys, dest, x1, mod_l, nf, tm, final):
    bsz, seq, d = x1.shape
    nt = seq // tm
    return pl.pallas_call(
        functools.partial(_combine_kernel, final=final),
        out_shape=jax.ShapeDtypeStruct((bsz, seq, d), F32),
        grid=(bsz, nt),
        in_specs=[
            pl.BlockSpec((None, 1, tm), lambda b, i: (b * nt + i, 0, 0),
                         memory_space=pltpu.SMEM),
            pl.BlockSpec((None, tm, d), lambda b, i: (b, i, 0)),
            pl.BlockSpec((None, N_MOD, d), lambda b, i: (b, 0, 0)),
            pl.BlockSpec((1, d), lambda b, i: (0, 0)),
            pl.BlockSpec(memory_space=pl.ANY),
        ],
        out_specs=pl.BlockSpec((None, tm, d), lambda b, i: (b, i, 0)),
        scratch_shapes=[pltpu.VMEM((tm, d), F32), pltpu.SemaphoreType.DMA],
        compiler_params=_cparams(("arbitrary", "arbitrary")),
        name="moe_combine",
    )(dest.reshape(bsz * nt, 1, tm), x1, mod_l, nf, ys)


def _rope_tables(seq):
    rows = seq // GRID_W
    row = jnp.broadcast_to(jnp.arange(rows, dtype=F32)[:, None], (rows, GRID_W)).reshape(seq)
    col = jnp.broadcast_to(jnp.arange(GRID_W, dtype=F32)[None, :], (rows, GRID_W)).reshape(seq)

    def cs(rot_dim):
        n = rot_dim // 4
        inv = 1.0 / (ROPE_THETA ** (jnp.arange(n, dtype=F32) / n))
        ang = jnp.concatenate([row[:, None] * inv, col[:, None] * inv], axis=-1)
        return jnp.cos(ang), jnp.sin(ang)

    def place(pieces):
        out = jnp.zeros((seq, LANES), F32)
        for off, val in pieces:
            out = out.at[:, off:off + val.shape[1]].set(val)
        return out

    ca, sa = cs(HEAD_DIM_A)
    ha = HEAD_DIM_A // 2
    scale_a = HEAD_DIM_A ** -0.5 * LOG2E
    ta = jnp.stack([place([(0, ca), (ha, ca)]), place([(0, -sa)]), place([(ha, sa)])]) * scale_a
    cb, sb = cs(QK_ROPE)
    hb = QK_ROPE // 2
    scale_b = (QK_NOPE + QK_ROPE) ** -0.5 * LOG2E
    ones = jnp.ones((seq, QK_NOPE), F32)
    cos_q = place([(0, ones), (QK_NOPE, cb), (QK_NOPE + hb, cb)]) * scale_b
    sin_lo = place([(QK_NOPE, -sb)])
    sin_hi = place([(QK_NOPE + hb, sb)])
    cos_k = place([(QK_NOPE, cb), (QK_NOPE + hb, cb)])
    tb = jnp.stack([cos_q, sin_lo * scale_b, sin_hi * scale_b, cos_k, sin_lo, sin_hi])
    return ta, tb


def _pad_heads(w, n_heads, width, offset=0):
    k = w.shape[0]
    w = w.reshape(k, n_heads, width)
    out = jnp.zeros((k, n_heads, LANES), w.dtype).at[:, :, offset:offset + width].set(w)
    return out.reshape(k, n_heads * LANES)


def _prep_layer(w_in, w_q_b, w_kv_b, w_proj_a, w_proj_b):
    o = 0
    parts = []
    sizes = (N_HEADS_A * HEAD_DIM_A, N_KV_A * HEAD_DIM_A, N_KV_A * HEAD_DIM_A,
             Q_LORA, KV_LORA, QK_ROPE, D_MODEL, D_MODEL)
    segs = []
    for s in sizes:
        segs.append(w_in[:, o:o + s])
        o += s
    parts = [
        _pad_heads(segs[0], N_HEADS_A, HEAD_DIM_A),
        _pad_heads(segs[1], N_KV_A, HEAD_DIM_A),
        _pad_heads(segs[2], N_KV_A, HEAD_DIM_A),
        segs[3], segs[4],
        _pad_heads(segs[5], 1, QK_ROPE, offset=QK_NOPE),
        segs[6], segs[7],
    ]
    win = jnp.concatenate(parts, axis=1).astype(BF16)
    wq = _pad_heads(w_q_b, N_HEADS_B, QK_NOPE + QK_ROPE).astype(BF16)
    wkv = w_kv_b.reshape(KV_LORA, N_HEADS_B, QK_NOPE + V_DIM_B)
    wk = _pad_heads(wkv[:, :, :QK_NOPE].reshape(KV_LORA, -1), N_HEADS_B, QK_NOPE).astype(BF16)
    wv = _pad_heads(wkv[:, :, QK_NOPE:].reshape(KV_LORA, -1), N_HEADS_B, V_DIM_B).astype(BF16)

    def pad_rows(w, n_heads, width):
        return _pad_heads(w.T, n_heads, width).T.astype(BF16)

    wpa = pad_rows(w_proj_a, N_HEADS_A, HEAD_DIM_A)
    wpb = pad_rows(w_proj_b, N_HEADS_B, V_DIM_B)
    return win, wq, wk, wv, wpa, wpb


def _pad_lanes(g, offset=0):
    return jnp.zeros((1, LANES), F32).at[0, offset:offset + g.shape[0]].set(g)


def kernel(x, c, w_ada, b_ada, norm1, w_in, q_norm_a, k_norm_a, q_a_norm, kv_norm, w_q_b, w_kv_b, w_proj_a, w_proj_b, w_o, norm2, w_router, router_bias, w1, w3, w2, norm_f):
    bsz, seq, d = x.shape
    depth = w_in.shape[0]
    t = bsz * seq
    tm = min(256, seq)
    tm_proj = min(512, seq)
    bm = min(256, t)
    tk = min(256, seq)

    mod = _modulation(c, w_ada, b_ada).reshape(depth, bsz, N_MOD, d)
    ta, tb = _rope_tables(seq)
    wr = jnp.zeros((d, LANES), F32).at[:, :N_EXPERTS].set(w_router)
    rb = router_bias.reshape(N_EXPERTS, 1)
    n_rows = ((t + bm - 1) // bm) * bm + N_BUCKETS * bm
    nb = n_rows // bm
    w1b, w3b, w2b = w1.astype(BF16), w3.astype(BF16), w2.astype(BF16)
    nf = norm_f.reshape(1, d)
    bucket_ids = jnp.arange(N_BUCKETS, dtype=jnp.int32)
    bucket_ea = (bucket_ids // N_PAIRS) * EXPERTS_PER_GROUP + jnp.array(PAIR_LO, jnp.int32)[bucket_ids % N_PAIRS]
    bucket_eb = (bucket_ids // N_PAIRS) * EXPERTS_PER_GROUP + jnp.array(PAIR_HI, jnp.int32)[bucket_ids % N_PAIRS]

    for l in range(depth):
        win, wq, wk, wv, wpa, wpb = _prep_layer(w_in[l], w_q_b[l], w_kv_b[l],
                                                w_proj_a[l], w_proj_b[l])
        qa, ka, va, qb, kb, vb, ga, gb = _inproj(
            x, mod[l], norm1[l].reshape(1, d), win,
            _pad_lanes(q_norm_a[l]), _pad_lanes(k_norm_a[l]),
            q_a_norm[l].reshape(1, Q_LORA), kv_norm[l].reshape(1, KV_LORA),
            wq, wk, wv, ta, tb, tm_proj)
        n_unroll = min(16, seq // tk)
        oa = _attention(qa, ka, va, group=GROUP_A, tq=min(256, seq), tk=tk, unroll=n_unroll,
                        chains=1)
        ob = _attention(qb, kb, vb, group=1, tq=min(1024, seq), tk=tk, unroll=n_unroll, chains=1)
        x1, h2, bucket, rank, cnt = _outproj(
            oa, ob, ga, gb, x, mod[l], norm2[l].reshape(1, d), wpa, wpb,
            w_o[l].astype(BF16), wr, rb, tm)

        counts = cnt[:, 0].astype(jnp.int32)
        padded = ((counts + bm - 1) // bm) * bm
        pends = jnp.cumsum(padded)
        pstarts = pends - padded
        dest = rank
        for e in range(N_BUCKETS):
            dest = dest + jnp.where(bucket == e, pstarts[e], 0)
        blk_start = jnp.arange(nb, dtype=jnp.int32) * bm
        blk_b = jnp.minimum(jnp.sum(blk_start[:, None] >= pends[None, :], axis=1),
                            N_BUCKETS - 1).astype(jnp.int32)
        blk_rows = jnp.clip(pstarts[blk_b] + counts[blk_b] - blk_start, 0, bm).astype(jnp.int32)
        n_valid = jnp.maximum(pends[-1:] // bm, 1).astype(jnp.int32)

        xs = _dispatch(h2.reshape(t, XS_COLS), dest, n_rows, tm)
        ys = _experts(xs, w1b, w3b, w2b, l, bucket_ea[blk_b], bucket_eb[blk_b], blk_rows,
                      n_valid, bm)
        x = _combine(ys, dest, x1, mod[l], nf, tm, final=(l == depth - 1))
    return x
```

```python
import functools

import jax
import jax.numpy as jnp
from jax import lax
from jax.experimental import pallas as pl
from jax.experimental.pallas import tpu as pltpu

F32 = jnp.float32
BF16 = jnp.bfloat16

D_MODEL = 1024
GRID_W = 64
ROPE_THETA = 10000.0
EPS = 1e-6
N_HEADS_A = 8
N_KV_A = 2
GROUP_A = N_HEADS_A // N_KV_A
HEAD_DIM_A = 64
N_HEADS_B = 8
Q_LORA = 384
KV_LORA = 256
QK_NOPE = 64
QK_ROPE = 32
V_DIM_B = 64
N_MOD = 6
N_EXPERTS = 16
N_GROUPS = 4
EXPERTS_PER_GROUP = N_EXPERTS // N_GROUPS
D_EXPERT = 512
N_PAIRS = EXPERTS_PER_GROUP * (EXPERTS_PER_GROUP - 1) // 2
N_BUCKETS = N_GROUPS * N_PAIRS
PAIR_LO = (0, 0, 0, 1, 1, 2)
PAIR_HI = (1, 2, 3, 2, 3, 3)

LANES = 128
ONES_LANE = 64
LOG2E = 1.4426950408889634
XS_COLS = D_MODEL + LANES
DMA_UNROLL = 8
VMEM_LIMIT = 56 * 1024 * 1024

_C_QA = 0
_C_KA = _C_QA + N_HEADS_A * LANES
_C_VA = _C_KA + N_KV_A * LANES
_C_QL = _C_VA + N_KV_A * LANES
_C_KVL = _C_QL + Q_LORA
_C_KR = _C_KVL + KV_LORA
_C_GA = _C_KR + LANES
_C_GB = _C_GA + D_MODEL
_C_END = _C_GB + D_MODEL


def _cparams(sem):
    return pltpu.CompilerParams(dimension_semantics=sem, vmem_limit_bytes=VMEM_LIMIT)


def _resident(a):
    return pl.BlockSpec(a.shape, lambda *_: (0,) * a.ndim, pipeline_mode=pl.Buffered(1))


def _split(a):
    hi = a.astype(BF16)
    lo = (a - hi.astype(F32)).astype(BF16)
    return hi, lo


def _dot(a, b):
    return jnp.dot(a, b, preferred_element_type=F32)


def _dot3(a, b):
    ah, al = _split(a)
    bh, bl = _split(b)
    return _dot(ah, bh) + (_dot(ah, bl) + _dot(al, bh))


def _mod_kernel(c_ref, w_ref, b_ref, o_ref):
    c = c_ref[...]
    cond = c * jax.nn.sigmoid(c)
    o_ref[...] = _dot3(cond, w_ref[...]) + b_ref[...]


def _modulation(c, w_ada, b_ada):
    depth, d, n = w_ada.shape
    bsz = c.shape[0]
    cb = 1536
    return pl.pallas_call(
        _mod_kernel,
        out_shape=jax.ShapeDtypeStruct((depth, bsz, n), F32),
        grid=(depth, n // cb),
        in_specs=[
            pl.BlockSpec((bsz, d), lambda l, j: (0, 0)),
            pl.BlockSpec((None, d, cb), lambda l, j: (l, 0, j)),
            pl.BlockSpec((None, 1, cb), lambda l, j: (l, 0, j)),
        ],
        out_specs=pl.BlockSpec((None, bsz, cb), lambda l, j: (l, 0, j)),
        compiler_params=_cparams(("parallel", "parallel")),
        name="adaln_mod",
    )(c, w_ada, b_ada.reshape(depth, 1, n))


def _rms(v, n):
    return v * lax.rsqrt(jnp.sum(v * v, axis=-1, keepdims=True) * (1.0 / n) + EPS)


def _rope(v, cos, sin_lo, sin_hi, shift):
    return (v * cos + pltpu.roll(v, LANES - shift, 1) * sin_lo
            + pltpu.roll(v, shift, 1) * sin_hi)


def _ones_lane(width):
    lane = lax.broadcasted_iota(jnp.int32, (1, width), 1)
    return jnp.where(lane % LANES == ONES_LANE, 1.0, 0.0)


def _inproj_kernel(x_ref, *refs):
    _inproj_body(x_ref[...], *refs)


def _inproj_combine_kernel(dcur_ref, dnext_ref, x1_ref, modp_ref, ys_ref, *refs):
    body_refs, (xnew_ref, buf, sems) = refs[:-3], refs[-3:]
    tm = x1_ref.shape[0]
    step = pl.program_id(0) * pl.num_programs(1) + pl.program_id(1)
    last = pl.num_programs(0) * pl.num_programs(1) - 1
    slot = step % 2

    def row_copy(dest_ref, t, s):
        return pltpu.make_async_copy(ys_ref.at[pl.ds(dest_ref[0, t], 1)],
                                     buf.at[s, pl.ds(t, 1)], sems.at[s])

    def tile_wait(s):
        pltpu.make_async_copy(ys_ref.at[pl.ds(0, tm)], buf.at[s], sems.at[s]).wait()

    @pl.when(step == 0)
    def _():
        def issue(t, c):
            row_copy(dcur_ref, t, 0).start()
            return c
        lax.fori_loop(0, tm, issue, 0, unroll=DMA_UNROLL)

    tile_wait(slot)
    x = x1_ref[...] + modp_ref[5:6, :] * buf[slot]
    xnew_ref[...] = x
    for t in range(tm):
        row_copy(dnext_ref, t, 1 - slot).start()
    _inproj_body(x, *body_refs)

    @pl.when(step == last)
    def _():
        tile_wait(1 - slot)


def _inproj_body(x, mod_ref, n1_ref, win_ref, gqa_ref, gka_ref, gql_ref, gkv_ref,
                 wq_ref, wk_ref, wv_ref, ta_ref, tb_ref,
                 qa_ref, ka_ref, va_ref, qb_ref, kb_ref, vb_ref, ga_ref, gb_ref):
    sh1 = mod_ref[0:1, :]
    sc1 = mod_ref[1:2, :]
    h = (_rms(x, D_MODEL) * n1_ref[...]) * (1.0 + sc1) + sh1
    hb = h.astype(BF16)

    cos_a, sin_a_lo, sin_a_hi = ta_ref[0], ta_ref[1], ta_ref[2]
    cos_b, sin_b_lo, sin_b_hi = tb_ref[0], tb_ref[1], tb_ref[2]
    cos_k, sin_k_lo, sin_k_hi = tb_ref[3], tb_ref[4], tb_ref[5]

    gqa = gqa_ref[...]
    z = _dot(hb, win_ref[:, _C_QA:_C_KA])
    for hh in range(N_HEADS_A):
        zn = _rms(z[:, hh * LANES:(hh + 1) * LANES], HEAD_DIM_A) * gqa
        qa_ref[:, hh * LANES:(hh + 1) * LANES] = _rope(
            zn, cos_a, sin_a_lo, sin_a_hi, HEAD_DIM_A // 2).astype(BF16)
    gka = gka_ref[...]
    z = _dot(hb, win_ref[:, _C_KA:_C_QL])
    for hh in range(N_KV_A):
        zn = _rms(z[:, hh * LANES:(hh + 1) * LANES], HEAD_DIM_A) * gka
        ka_ref[:, hh * LANES:(hh + 1) * LANES] = _rope(
            zn, cos_a, sin_a_lo, sin_a_hi, HEAD_DIM_A // 2).astype(BF16)
    va_ref[...] = (z[:, N_KV_A * LANES:] + _ones_lane(N_KV_A * LANES)).astype(BF16)

    z = _dot(hb, win_ref[:, _C_QL:_C_GA])
    qlb = (_rms(z[:, 0:Q_LORA], Q_LORA) * gql_ref[...]).astype(BF16)
    kvb = (_rms(z[:, Q_LORA:Q_LORA + KV_LORA], KV_LORA) * gkv_ref[...]).astype(BF16)
    kr = _rope(z[:, Q_LORA + KV_LORA:], cos_k, sin_k_lo, sin_k_hi, QK_ROPE // 2)
    zq = _dot(qlb, wq_ref[...])
    for hh in range(N_HEADS_B):
        qb_ref[:, hh * LANES:(hh + 1) * LANES] = _rope(
            zq[:, hh * LANES:(hh + 1) * LANES], cos_b, sin_b_lo, sin_b_hi,
            QK_ROPE // 2).astype(BF16)

    zk = _dot(kvb, wk_ref[...])
    for hh in range(N_HEADS_B):
        kb_ref[:, hh * LANES:(hh + 1) * LANES] = (zk[:, hh * LANES:(hh + 1) * LANES] + kr).astype(BF16)
    vb_ref[...] = (_dot(kvb, wv_ref[...]) + _ones_lane(N_HEADS_B * LANES)).astype(BF16)

    ga_ref[...] = jax.nn.sigmoid(_dot(hb, win_ref[:, _C_GA:_C_GB])).astype(BF16)
    gb_ref[...] = jax.nn.sigmoid(_dot(hb, win_ref[:, _C_GB:_C_END])).astype(BF16)


def _inproj(x, mod_l, n1, win, gqa, gka, gql, gkv, wq, wk, wv, ta, tb, tm, moe=None):
    bsz, seq, d = x.shape
    nt = seq // tm
    row = lambda w: pl.BlockSpec((None, tm, w), lambda b, i: (b, i, 0))
    full = _resident
    widths = (N_HEADS_A * LANES, N_KV_A * LANES, N_KV_A * LANES,
              N_HEADS_B * LANES, N_HEADS_B * LANES, N_HEADS_B * LANES, D_MODEL, D_MODEL)
    mod_spec = pl.BlockSpec((None, N_MOD, d), lambda b, i: (b, 0, 0))
    body_specs = [
        mod_spec,
        full(n1), full(win), full(gqa), full(gka), full(gql), full(gkv),
        full(wq), full(wk), full(wv),
        pl.BlockSpec((3, tm, LANES), lambda b, i: (0, i, 0)),
        pl.BlockSpec((6, tm, LANES), lambda b, i: (0, i, 0)),
    ]
    body_args = (mod_l, n1, win, gqa, gka, gql, gkv, wq, wk, wv, ta, tb)
    out_shape = [jax.ShapeDtypeStruct((bsz, seq, w), BF16) for w in widths]
    out_specs = [row(w) for w in widths]
    if moe is None:
        return pl.pallas_call(
            _inproj_kernel,
            out_shape=out_shape,
            grid=(bsz, nt),
            in_specs=[row(d)] + body_specs,
            out_specs=out_specs,
            compiler_params=_cparams(("parallel", "parallel")),
            name="inproj",
        )(x, *body_args)
    ys, dest, mod_prev = moe
    dest_t = dest.reshape(bsz * nt, 1, tm)
    n_tiles = bsz * nt
    return pl.pallas_call(
        _inproj_combine_kernel,
        out_shape=out_shape + [jax.ShapeDtypeStruct((bsz, seq, d), F32)],
        grid=(bsz, nt),
        in_specs=[
            pl.BlockSpec((None, 1, tm), lambda b, i: (b * nt + i, 0, 0), memory_space=pltpu.SMEM),
            pl.BlockSpec((None, 1, tm), lambda b, i: (jnp.minimum(b * nt + i + 1, n_tiles - 1), 0, 0),
                         memory_space=pltpu.SMEM),
            row(d), mod_spec,
            pl.BlockSpec(memory_space=pl.ANY),
        ] + body_specs,
        out_specs=out_specs + [row(d)],
        scratch_shapes=[pltpu.VMEM((2, tm, d), F32), pltpu.SemaphoreType.DMA((2,))],
        compiler_params=_cparams(("arbitrary", "arbitrary")),
        name="combine_inproj",
    )(dest_t, dest_t, x, mod_prev, ys, *body_args)


def _attn_kernel(q_ref, k_ref, v_ref, o_ref, *, group, tk, unroll):
    tq = q_ref.shape[0]
    seq = k_ref.shape[0]
    if group == 1:
        q = q_ref[...]
    else:
        q = jnp.concatenate(
            [q_ref[:, g * LANES:(g + 1) * LANES] for g in range(group)], axis=0)
    m_rows = group * tq

    def body(j, carry):
        m, acc = carry
        start = pl.multiple_of(j * tk, tk)
        k = k_ref[pl.ds(start, tk), :]
        v = v_ref[pl.ds(start, tk), :]
        s = lax.dot_general(q, k, (((1,), (1,)), ((), ())), preferred_element_type=F32)
        m_new = jnp.maximum(m, jnp.max(s, axis=-1, keepdims=True))
        alpha = jnp.exp2(m - m_new)
        p = jnp.exp2((s - m_new).astype(BF16))
        return m_new, alpha * acc + _dot(p, v)

    n_blk = seq // tk
    init = (jnp.full((m_rows, 1), -jnp.inf, F32), jnp.zeros((m_rows, LANES), F32))
    _, acc = lax.fori_loop(0, n_blk, body, init, unroll=unroll if n_blk % unroll == 0 else 1)
    o = (acc / acc[:, ONES_LANE:ONES_LANE + 1]).astype(o_ref.dtype)
    for g in range(group):
        o_ref[:, g * LANES:(g + 1) * LANES] = o[g * tq:(g + 1) * tq, :]


def _attention(q, k, v, *, group, tq, tk, unroll):
    bsz, seq, qw = q.shape
    n_kv = k.shape[-1] // LANES
    return pl.pallas_call(
        functools.partial(_attn_kernel, group=group, tk=tk, unroll=unroll),
        out_shape=jax.ShapeDtypeStruct((bsz, seq, qw), BF16),
        grid=(bsz, n_kv, seq // tq),
        in_specs=[
            pl.BlockSpec((None, tq, group * LANES), lambda b, h, i: (b, i, h)),
            pl.BlockSpec((None, seq, LANES), lambda b, h, i: (b, 0, h)),
            pl.BlockSpec((None, seq, LANES), lambda b, h, i: (b, 0, h)),
        ],
        out_specs=pl.BlockSpec((None, tq, group * LANES), lambda b, h, i: (b, i, h)),
        compiler_params=_cparams(("parallel", "parallel", "parallel")),
        name="flash_attn_g%d" % group,
    )(q, k, v)


def _first_argmax(vals):
    best, idx = vals[0], jnp.zeros_like(vals[0], dtype=jnp.int32)
    for i in range(1, len(vals)):
        gt = vals[i] > best
        best = jnp.where(gt, vals[i], best)
        idx = jnp.where(gt, i, idx)
    return best, idx


def _pick(vals, idx):
    out = vals[0]
    for i in range(1, len(vals)):
        out = jnp.where(idx == i, vals[i], out)
    return out


def _outproj_kernel(oa_ref, ob_ref, ga_ref, gb_ref, x_ref, mod_ref, n2_ref,
                    wpa_ref, wpb_ref, wo_ref, wr_ref, rb_ref,
                    x1_ref, h2_ref, bkt_ref, rank_ref, cnt_ref, base_ref, gwt_ref):
    tm = x_ref.shape[0]

    @pl.when((pl.program_id(0) == 0) & (pl.program_id(1) == 0))
    def _():
        base_ref[...] = jnp.zeros_like(base_ref)

    ya = _dot(oa_ref[...], wpa_ref[...])
    yb = _dot(ob_ref[...], wpb_ref[...])
    merged = ga_ref[...].astype(F32) * ya + gb_ref[...].astype(F32) * yb
    att = _dot(merged.astype(BF16), wo_ref[...])
    g1 = mod_ref[2:3, :]
    sh2 = mod_ref[3:4, :]
    sc2 = mod_ref[4:5, :]
    x1 = x_ref[...] + g1 * att
    x1_ref[...] = x1
    h2 = (_rms(x1, D_MODEL) * n2_ref[...]) * (1.0 + sc2) + sh2
    h2_ref[:, 0:D_MODEL] = h2

    logits = _dot3(h2, wr_ref[...])
    lt = logits.T[0:N_EXPERTS, :]
    scores = jax.nn.sigmoid(lt)
    biased = scores + rb_ref[...]
    brow = [biased[e:e + 1, :] for e in range(N_EXPERTS)]
    srow = [scores[e:e + 1, :] for e in range(N_EXPERTS)]
    grp = []
    for g in range(N_GROUPS):
        a, b, c, d = brow[4 * g:4 * g + 4]
        hi1, lo1 = jnp.maximum(a, b), jnp.minimum(a, b)
        hi2, lo2 = jnp.maximum(c, d), jnp.minimum(c, d)
        grp.append(jnp.maximum(hi1, hi2) + jnp.maximum(jnp.minimum(hi1, hi2),
                                                       jnp.maximum(lo1, lo2)))
    _, gsel = _first_argmax(grp)
    bsel = [_pick([brow[4 * g + i] for g in range(N_GROUPS)], gsel)
            for i in range(EXPERTS_PER_GROUP)]
    ssel = [_pick([srow[4 * g + i] for g in range(N_GROUPS)], gsel)
            for i in range(EXPERTS_PER_GROUP)]
    _, i0 = _first_argmax(bsel)
    _, i1 = _first_argmax([jnp.where(i0 == i, -jnp.inf, bsel[i])
                           for i in range(EXPERTS_PER_GROUP)])
    w0 = _pick(ssel, i0)
    w1 = _pick(ssel, i1)
    wsum = w0 + w1
    first_lo = i0 < i1
    lo = jnp.where(first_lo, i0, i1)
    hi = jnp.where(first_lo, i1, i0)
    w_lo = jnp.where(first_lo, w0, w1) / wsum
    w_hi = jnp.where(first_lo, w1, w0) / wsum
    pair = jnp.where(lo == 0, 0, jnp.where(lo == 1, 3, 5)) + (hi - lo - 1)
    bucket = gsel * N_PAIRS + pair
    bkt_ref[...] = bucket

    gwt_ref[...] = jnp.zeros_like(gwt_ref)
    gwt_ref[0:1, :] = w_lo
    gwt_ref[1:2, :] = w_hi
    h2_ref[:, D_MODEL:] = gwt_ref[...].T

    onehot = lax.broadcasted_iota(jnp.int32, (N_BUCKETS, tm), 0) == bucket
    cnt = jnp.where(onehot, 1.0, 0.0)
    r_i = lax.broadcasted_iota(jnp.int32, (tm, tm), 0)
    c_i = lax.broadcasted_iota(jnp.int32, (tm, tm), 1)
    upper = jnp.where(r_i < c_i, 1.0, 0.0).astype(BF16)
    pref = _dot(cnt.astype(BF16), upper) + base_ref[:, 0:1]
    rank_ref[...] = jnp.sum(jnp.where(onehot, pref, 0.0), axis=0, keepdims=True).astype(jnp.int32)
    base_ref[...] = base_ref[...] + jnp.sum(cnt, axis=1, keepdims=True)
    cnt_ref[...] = base_ref[...]


def _outproj(oa, ob, ga, gb, x, mod_l, n2, wpa, wpb, wo, wr, rb, tm):
    bsz, seq, d = x.shape
    nt = seq // tm
    row = lambda w: pl.BlockSpec((None, tm, w), lambda b, i: (b, i, 0))
    full = _resident
    tok = pl.BlockSpec((1, tm), lambda b, i: (0, b * nt + i))
    t = bsz * seq
    return pl.pallas_call(
        _outproj_kernel,
        out_shape=[
            jax.ShapeDtypeStruct((bsz, seq, d), F32),
            jax.ShapeDtypeStruct((bsz, seq, XS_COLS), F32),
            jax.ShapeDtypeStruct((1, t), jnp.int32),
            jax.ShapeDtypeStruct((1, t), jnp.int32),
            jax.ShapeDtypeStruct((N_BUCKETS, LANES), F32),
        ],
        grid=(bsz, nt),
        in_specs=[
            row(d), row(d), row(d), row(d), row(d),
            pl.BlockSpec((None, N_MOD, d), lambda b, i: (b, 0, 0)),
            full(n2), full(wpa), full(wpb), full(wo), full(wr), full(rb),
        ],
        out_specs=[row(d), row(XS_COLS), tok, tok,
                   pl.BlockSpec((N_BUCKETS, LANES), lambda b, i: (0, 0))],
        scratch_shapes=[pltpu.VMEM((N_BUCKETS, LANES), F32), pltpu.VMEM((LANES, tm), F32)],
        compiler_params=_cparams(("arbitrary", "arbitrary")),
        name="outproj_router",
    )(oa, ob, ga, gb, x, mod_l, n2, wpa, wpb, wo, wr, rb)


def _dispatch_kernel(dest_ref, h_ref, xs_ref, sem):
    tm = h_ref.shape[0]

    def issue(t, c):
        pltpu.make_async_copy(h_ref.at[pl.ds(t, 1)], xs_ref.at[pl.ds(dest_ref[0, t], 1)],
                              sem).start()
        return c

    lax.fori_loop(0, tm, issue, 0, unroll=DMA_UNROLL)
    pltpu.make_async_copy(h_ref, xs_ref.at[pl.ds(0, tm)], sem).wait()


def _dispatch(h2, dest, n_rows, tm):
    t, d = h2.shape
    nt = t // tm
    return pl.pallas_call(
        _dispatch_kernel,
        out_shape=jax.ShapeDtypeStruct((n_rows, d), F32),
        grid=(nt,),
        in_specs=[
            pl.BlockSpec((None, 1, tm), lambda i: (i, 0, 0), memory_space=pltpu.SMEM),
            pl.BlockSpec((tm, d), lambda i: (i, 0)),
        ],
        out_specs=pl.BlockSpec(memory_space=pl.ANY),
        scratch_shapes=[pltpu.SemaphoreType.DMA],
        compiler_params=_cparams(("arbitrary",)),
        name="moe_dispatch",
    )(dest.reshape(nt, 1, tm), h2)


def _expert_kernel(ea_ref, eb_ref, br_ref, nv_ref, x_ref, w1a_ref, w3a_ref, w2a_ref,
                   w1b_ref, w3b_ref, w2b_ref, y_ref):
    b = pl.program_id(0)

    @pl.when(b < nv_ref[0])
    def _():
        bm = x_ref.shape[0]
        live = lax.broadcasted_iota(jnp.int32, (bm, 1), 0) < br_ref[b]
        x = jnp.where(live, x_ref[:, 0:D_MODEL], 0.0).astype(BF16)
        gates = jnp.where(live, x_ref[:, D_MODEL:], 0.0)

        def hidden(w1_ref, w3_ref, gate):
            a = _dot(x, w1_ref[...])
            g = _dot(x, w3_ref[...])
            return ((a * jax.nn.sigmoid(a)) * g * gate).astype(BF16)

        h_lo = hidden(w1a_ref, w3a_ref, gates[:, 0:1])
        h_hi = hidden(w1b_ref, w3b_ref, gates[:, 1:2])
        y_ref[...] = _dot(h_lo, w2a_ref[...]) + _dot(h_hi, w2b_ref[...])


def _experts(xs, w1, w3, w2, layer, blk_ea, blk_eb, blk_rows, n_valid, bm):
    n_rows = xs.shape[0]
    nb = n_rows // bm
    d, f = w1.shape[-2:]
    last = lambda b, nv: jnp.minimum(b, nv[0] - 1)
    xmap = lambda b, ea, eb, br, nv: (last(b, nv), 0)
    amap = lambda b, ea, eb, br, nv: (layer, ea[last(b, nv)], 0, 0)
    bmap = lambda b, ea, eb, br, nv: (layer, eb[last(b, nv)], 0, 0)
    return pl.pallas_call(
        _expert_kernel,
        out_shape=jax.ShapeDtypeStruct((n_rows, d), F32),
        grid_spec=pltpu.PrefetchScalarGridSpec(
            num_scalar_prefetch=4,
            grid=(nb,),
            in_specs=[
                pl.BlockSpec((bm, XS_COLS), xmap),
                pl.BlockSpec((None, None, d, f), amap),
                pl.BlockSpec((None, None, d, f), amap),
                pl.BlockSpec((None, None, f, d), amap),
                pl.BlockSpec((None, None, d, f), bmap),
                pl.BlockSpec((None, None, d, f), bmap),
                pl.BlockSpec((None, None, f, d), bmap),
            ],
            out_specs=pl.BlockSpec((bm, d), xmap),
        ),
        compiler_params=_cparams(("arbitrary",)),
        name="moe_experts",
    )(blk_ea, blk_eb, blk_rows, n_valid, xs, w1, w3, w2, w1, w3, w2)


def _combine_kernel(dest_ref, x1_ref, mod_ref, nf_ref, ys_ref, o_ref, buf, sem):
    tm = x1_ref.shape[0]

    def issue(t, c):
        pltpu.make_async_copy(ys_ref.at[pl.ds(dest_ref[0, t], 1)], buf.at[pl.ds(t, 1)],
                              sem).start()
        return c

    lax.fori_loop(0, tm, issue, 0, unroll=DMA_UNROLL)
    pltpu.make_async_copy(ys_ref.at[pl.ds(0, tm)], buf, sem).wait()
    x2 = x1_ref[...] + mod_ref[5:6, :] * buf[...]
    o_ref[...] = _rms(x2, D_MODEL) * nf_ref[...]


def _combine(ys, dest, x1, mod_l, nf, tm):
    bsz, seq, d = x1.shape
    nt = seq // tm
    return pl.pallas_call(
        _combine_kernel,
        out_shape=jax.ShapeDtypeStruct((bsz, seq, d), F32),
        grid=(bsz, nt),
        in_specs=[
            pl.BlockSpec((None, 1, tm), lambda b, i: (b * nt + i, 0, 0),
                         memory_space=pltpu.SMEM),
            pl.BlockSpec((None, tm, d), lambda b, i: (b, i, 0)),
            pl.BlockSpec((None, N_MOD, d), lambda b, i: (b, 0, 0)),
            pl.BlockSpec((1, d), lambda b, i: (0, 0)),
            pl.BlockSpec(memory_space=pl.ANY),
        ],
        out_specs=pl.BlockSpec((None, tm, d), lambda b, i: (b, i, 0)),
        scratch_shapes=[pltpu.VMEM((tm, d), F32), pltpu.SemaphoreType.DMA],
        compiler_params=_cparams(("arbitrary", "arbitrary")),
        name="moe_combine",
    )(dest.reshape(bsz * nt, 1, tm), x1, mod_l, nf, ys)


def _rope_tables(seq):
    rows = seq // GRID_W
    row = jnp.broadcast_to(jnp.arange(rows, dtype=F32)[:, None], (rows, GRID_W)).reshape(seq)
    col = jnp.broadcast_to(jnp.arange(GRID_W, dtype=F32)[None, :], (rows, GRID_W)).reshape(seq)

    def cs(rot_dim):
        n = rot_dim // 4
        inv = 1.0 / (ROPE_THETA ** (jnp.arange(n, dtype=F32) / n))
        ang = jnp.concatenate([row[:, None] * inv, col[:, None] * inv], axis=-1)
        return jnp.cos(ang), jnp.sin(ang)

    def place(pieces):
        out = jnp.zeros((seq, LANES), F32)
        for off, val in pieces:
            out = out.at[:, off:off + val.shape[1]].set(val)
        return out

    ca, sa = cs(HEAD_DIM_A)
    ha = HEAD_DIM_A // 2
    scale_a = HEAD_DIM_A ** -0.5 * LOG2E
    ta = jnp.stack([place([(0, ca), (ha, ca)]), place([(0, -sa)]), place([(ha, sa)])]) * scale_a
    cb, sb = cs(QK_ROPE)
    hb = QK_ROPE // 2
    scale_b = (QK_NOPE + QK_ROPE) ** -0.5 * LOG2E
    ones = jnp.ones((seq, QK_NOPE), F32)
    cos_q = place([(0, ones), (QK_NOPE, cb), (QK_NOPE + hb, cb)]) * scale_b
    sin_lo = place([(QK_NOPE, -sb)])
    sin_hi = place([(QK_NOPE + hb, sb)])
    cos_k = place([(QK_NOPE, cb), (QK_NOPE + hb, cb)])
    tb = jnp.stack([cos_q, sin_lo * scale_b, sin_hi * scale_b, cos_k, sin_lo, sin_hi])
    return ta, tb


def _pad_heads(w, n_heads, width, offset=0):
    k = w.shape[0]
    w = w.reshape(k, n_heads, width)
    out = jnp.zeros((k, n_heads, LANES), w.dtype).at[:, :, offset:offset + width].set(w)
    return out.reshape(k, n_heads * LANES)


def _prep_layer(w_in, w_q_b, w_kv_b, w_proj_a, w_proj_b):
    o = 0
    parts = []
    sizes = (N_HEADS_A * HEAD_DIM_A, N_KV_A * HEAD_DIM_A, N_KV_A * HEAD_DIM_A,
             Q_LORA, KV_LORA, QK_ROPE, D_MODEL, D_MODEL)
    segs = []
    for s in sizes:
        segs.append(w_in[:, o:o + s])
        o += s
    parts = [
        _pad_heads(segs[0], N_HEADS_A, HEAD_DIM_A),
        _pad_heads(segs[1], N_KV_A, HEAD_DIM_A),
        _pad_heads(segs[2], N_KV_A, HEAD_DIM_A),
        segs[3], segs[4],
        _pad_heads(segs[5], 1, QK_ROPE, offset=QK_NOPE),
        segs[6], segs[7],
    ]
    win = jnp.concatenate(parts, axis=1).astype(BF16)
    wq = _pad_heads(w_q_b, N_HEADS_B, QK_NOPE + QK_ROPE).astype(BF16)
    wkv = w_kv_b.reshape(KV_LORA, N_HEADS_B, QK_NOPE + V_DIM_B)
    wk = _pad_heads(wkv[:, :, :QK_NOPE].reshape(KV_LORA, -1), N_HEADS_B, QK_NOPE).astype(BF16)
    wv = _pad_heads(wkv[:, :, QK_NOPE:].reshape(KV_LORA, -1), N_HEADS_B, V_DIM_B).astype(BF16)

    def pad_rows(w, n_heads, width):
        return _pad_heads(w.T, n_heads, width).T.astype(BF16)

    wpa = pad_rows(w_proj_a, N_HEADS_A, HEAD_DIM_A)
    wpb = pad_rows(w_proj_b, N_HEADS_B, V_DIM_B)
    return win, wq, wk, wv, wpa, wpb


def _pad_lanes(g, offset=0):
    return jnp.zeros((1, LANES), F32).at[0, offset:offset + g.shape[0]].set(g)


def kernel(x, c, w_ada, b_ada, norm1, w_in, q_norm_a, k_norm_a, q_a_norm, kv_norm, w_q_b, w_kv_b, w_proj_a, w_proj_b, w_o, norm2, w_router, router_bias, w1, w3, w2, norm_f):
    bsz, seq, d = x.shape
    depth = w_in.shape[0]
    t = bsz * seq
    tm = min(256, seq)
    tm_proj = min(512, seq)
    bm = min(256, t)
    tk = min(256, seq)

    mod = _modulation(c, w_ada, b_ada).reshape(depth, bsz, N_MOD, d)
    ta, tb = _rope_tables(seq)
    wr = jnp.zeros((d, LANES), F32).at[:, :N_EXPERTS].set(w_router)
    rb = router_bias.reshape(N_EXPERTS, 1)
    n_rows = ((t + bm - 1) // bm) * bm + N_BUCKETS * bm
    nb = n_rows // bm
    w1b, w3b, w2b = w1.astype(BF16), w3.astype(BF16), w2.astype(BF16)
    nf = norm_f.reshape(1, d)
    bucket_ids = jnp.arange(N_BUCKETS, dtype=jnp.int32)
    bucket_ea = (bucket_ids // N_PAIRS) * EXPERTS_PER_GROUP + jnp.array(PAIR_LO, jnp.int32)[bucket_ids % N_PAIRS]
    bucket_eb = (bucket_ids // N_PAIRS) * EXPERTS_PER_GROUP + jnp.array(PAIR_HI, jnp.int32)[bucket_ids % N_PAIRS]

    moe = None
    for l in range(depth):
        win, wq, wk, wv, wpa, wpb = _prep_layer(w_in[l], w_q_b[l], w_kv_b[l],
                                                w_proj_a[l], w_proj_b[l])
        outs = _inproj(
            x, mod[l], norm1[l].reshape(1, d), win,
            _pad_lanes(q_norm_a[l]), _pad_lanes(k_norm_a[l]),
            q_a_norm[l].reshape(1, Q_LORA), kv_norm[l].reshape(1, KV_LORA),
            wq, wk, wv, ta, tb, tm_proj, moe=moe)
        qa, ka, va, qb, kb, vb, ga, gb = outs[:8]
        if moe is not None:
            x = outs[8]
        n_unroll = min(16, seq // tk)
        oa = _attention(qa, ka, va, group=GROUP_A, tq=min(256, seq), tk=tk, unroll=n_unroll)
        ob = _attention(qb, kb, vb, group=1, tq=min(1024, seq), tk=tk, unroll=n_unroll)
        x1, h2, bucket, rank, cnt = _outproj(
            oa, ob, ga, gb, x, mod[l], norm2[l].reshape(1, d), wpa, wpb,
            w_o[l].astype(BF16), wr, rb, tm)

        counts = cnt[:, 0].astype(jnp.int32)
        padded = ((counts + bm - 1) // bm) * bm
        pends = jnp.cumsum(padded)
        pstarts = pends - padded
        dest = rank
        for e in range(N_BUCKETS):
            dest = dest + jnp.where(bucket == e, pstarts[e], 0)
        blk_start = jnp.arange(nb, dtype=jnp.int32) * bm
        blk_b = jnp.minimum(jnp.sum(blk_start[:, None] >= pends[None, :], axis=1),
                            N_BUCKETS - 1).astype(jnp.int32)
        blk_rows = jnp.clip(pstarts[blk_b] + counts[blk_b] - blk_start, 0, bm).astype(jnp.int32)
        n_valid = jnp.maximum(pends[-1:] // bm, 1).astype(jnp.int32)

        xs = _dispatch(h2.reshape(t, XS_COLS), dest, n_rows, tm)
        ys = _experts(xs, w1b, w3b, w2b, l, bucket_ea[blk_b], bucket_eb[blk_b], blk_rows,
                      n_valid, bm)
        x, moe = x1, (ys, dest, mod[l])
    return _combine(ys, dest, x1, mod[depth - 1], nf, tm)
```

```python
import functools

import jax
import jax.numpy as jnp
from jax import lax
from jax.experimental import pallas as pl
from jax.experimental.pallas import tpu as pltpu

F32 = jnp.float32
BF16 = jnp.bfloat16

D_MODEL = 1024
GRID_W = 64
ROPE_THETA = 10000.0
EPS = 1e-6
N_HEADS_A = 8
N_KV_A = 2
GROUP_A = N_HEADS_A // N_KV_A
HEAD_DIM_A = 64
N_HEADS_B = 8
Q_LORA = 384
KV_LORA = 256
QK_NOPE = 64
QK_ROPE = 32
V_DIM_B = 64
N_MOD = 6
N_EXPERTS = 16
N_GROUPS = 4
EXPERTS_PER_GROUP = N_EXPERTS // N_GROUPS
D_EXPERT = 512
N_PAIRS = EXPERTS_PER_GROUP * (EXPERTS_PER_GROUP - 1) // 2
N_BUCKETS = N_GROUPS * N_PAIRS
PAIR_LO = (0, 0, 0, 1, 1, 2)
PAIR_HI = (1, 2, 3, 2, 3, 3)

LANES = 128
ONES_LANE = 64
LOG2E = 1.4426950408889634
U32 = jnp.uint32
XS_HALF = D_MODEL // 2
XS_COLS = XS_HALF + LANES
DMA_UNROLL = 8
VMEM_LIMIT = 56 * 1024 * 1024

_C_QA = 0
_C_KA = _C_QA + N_HEADS_A * HEAD_DIM_A
_C_VA = _C_KA + N_KV_A * LANES
_C_QL = _C_VA + N_KV_A * LANES
_C_KVL = _C_QL + Q_LORA
_C_KR = _C_KVL + KV_LORA
_C_GA = _C_KR + LANES
_C_GB = _C_GA + D_MODEL
_C_END = _C_GB + D_MODEL


def _cparams(sem):
    return pltpu.CompilerParams(dimension_semantics=sem, vmem_limit_bytes=VMEM_LIMIT)


def _resident(a):
    return pl.BlockSpec(a.shape, lambda *_: (0,) * a.ndim, pipeline_mode=pl.Buffered(1))


def _split(a):
    hi = a.astype(BF16)
    lo = (a - hi.astype(F32)).astype(BF16)
    return hi, lo


def _dot(a, b):
    return jnp.dot(a, b, preferred_element_type=F32)


def _dot3(a, b):
    ah, al = _split(a)
    bh, bl = _split(b)
    return _dot(ah, bh) + (_dot(ah, bl) + _dot(al, bh))


def _mod_kernel(c_ref, w_ref, b_ref, o_ref):
    c = c_ref[...]
    cond = c * jax.nn.sigmoid(c)
    o_ref[...] = _dot3(cond, w_ref[...]) + b_ref[...]


def _modulation(c, w_ada, b_ada):
    depth, d, n = w_ada.shape
    bsz = c.shape[0]
    cb = 1536
    return pl.pallas_call(
        _mod_kernel,
        out_shape=jax.ShapeDtypeStruct((depth, bsz, n), F32),
        grid=(depth, n // cb),
        in_specs=[
            pl.BlockSpec((bsz, d), lambda l, j: (0, 0)),
            pl.BlockSpec((None, d, cb), lambda l, j: (l, 0, j)),
            pl.BlockSpec((None, 1, cb), lambda l, j: (l, 0, j)),
        ],
        out_specs=pl.BlockSpec((None, bsz, cb), lambda l, j: (l, 0, j)),
        compiler_params=_cparams(("parallel", "parallel")),
        name="adaln_mod",
    )(c, w_ada, b_ada.reshape(depth, 1, n))


def _rms(v, n):
    return v * lax.rsqrt(jnp.sum(v * v, axis=-1, keepdims=True) * (1.0 / n) + EPS)


def _rope(v, cos, sin_lo, sin_hi, shift):
    return (v * cos + pltpu.roll(v, LANES - shift, 1) * sin_lo
            + pltpu.roll(v, shift, 1) * sin_hi)


def _ones_lane(width):
    lane = lax.broadcasted_iota(jnp.int32, (1, width), 1)
    return jnp.where(lane % LANES == ONES_LANE, 1.0, 0.0)


def _inproj_kernel(x_ref, *refs):
    _inproj_body(x_ref[...], *refs)


def _inproj_combine_kernel(dcur_ref, dnext_ref, x1_ref, modp_ref, ys_ref, *refs):
    body_refs, (xnew_ref, buf, sems) = refs[:-3], refs[-3:]
    tm = x1_ref.shape[0]
    step = pl.program_id(0) * pl.num_programs(1) + pl.program_id(1)
    last = pl.num_programs(0) * pl.num_programs(1) - 1
    slot = step % 2

    def row_copy(dest_ref, t, s):
        return pltpu.make_async_copy(ys_ref.at[pl.ds(dest_ref[0, t], 1)],
                                     buf.at[s, pl.ds(t, 1)], sems.at[s])

    def tile_wait(s):
        pltpu.make_async_copy(ys_ref.at[pl.ds(0, tm)], buf.at[s], sems.at[s]).wait()

    @pl.when(step == 0)
    def _():
        def issue(t, c):
            row_copy(dcur_ref, t, 0).start()
            return c
        lax.fori_loop(0, tm, issue, 0, unroll=DMA_UNROLL)

    tile_wait(slot)
    x = x1_ref[...] + modp_ref[5:6, :] * buf[slot]
    xnew_ref[...] = x
    for t in range(tm):
        row_copy(dnext_ref, t, 1 - slot).start()
    _inproj_body(x, *body_refs)

    @pl.when(step == last)
    def _():
        tile_wait(1 - slot)


def _inproj_body(x, mod_ref, n1_ref, win_ref, gqa_ref, gka_ref, gql_ref, gkv_ref,
                 wq_ref, wk_ref, wv_ref, ta_ref, tb_ref,
                 qa_ref, ka_ref, va_ref, qb_ref, kb_ref, vb_ref, ga_ref, gb_ref):
    sh1 = mod_ref[0:1, :]
    sc1 = mod_ref[1:2, :]
    h = (_rms(x, D_MODEL) * n1_ref[...]) * (1.0 + sc1) + sh1
    hb = h.astype(BF16)

    cos_a, sin_a_lo, sin_a_hi = ta_ref[0], ta_ref[1], ta_ref[2]
    cos_b, sin_b_lo, sin_b_hi = tb_ref[0], tb_ref[1], tb_ref[2]
    cos_k, sin_k_lo, sin_k_hi = tb_ref[3], tb_ref[4], tb_ref[5]

    gqa = gqa_ref[...]
    z = _dot(hb, win_ref[:, _C_QA:_C_KA])
    low_half = lax.broadcasted_iota(jnp.int32, (1, LANES), 1) < HEAD_DIM_A
    for hh in range(N_HEADS_A):
        pair = z[:, (hh // 2) * LANES:(hh // 2 + 1) * LANES]
        if hh % 2:
            pair = pltpu.roll(pair, HEAD_DIM_A, 1)
        zn = _rms(jnp.where(low_half, pair, 0.0), HEAD_DIM_A) * gqa
        qa_ref[:, hh * LANES:(hh + 1) * LANES] = _rope(
            zn, cos_a, sin_a_lo, sin_a_hi, HEAD_DIM_A // 2).astype(BF16)
    gka = gka_ref[...]
    z = _dot(hb, win_ref[:, _C_KA:_C_QL])
    for hh in range(N_KV_A):
        zn = _rms(z[:, hh * LANES:(hh + 1) * LANES], HEAD_DIM_A) * gka
        ka_ref[:, hh * LANES:(hh + 1) * LANES] = _rope(
            zn, cos_a, sin_a_lo, sin_a_hi, HEAD_DIM_A // 2).astype(BF16)
    va_ref[...] = (z[:, N_KV_A * LANES:] + _ones_lane(N_KV_A * LANES)).astype(BF16)

    z = _dot(hb, win_ref[:, _C_QL:_C_GA])
    qlb = (_rms(z[:, 0:Q_LORA], Q_LORA) * gql_ref[...]).astype(BF16)
    kvb = (_rms(z[:, Q_LORA:Q_LORA + KV_LORA], KV_LORA) * gkv_ref[...]).astype(BF16)
    kr = _rope(z[:, Q_LORA + KV_LORA:], cos_k, sin_k_lo, sin_k_hi, QK_ROPE // 2)
    zq = _dot(qlb, wq_ref[...])
    for hh in range(N_HEADS_B):
        qb_ref[:, hh * LANES:(hh + 1) * LANES] = _rope(
            zq[:, hh * LANES:(hh + 1) * LANES], cos_b, sin_b_lo, sin_b_hi,
            QK_ROPE // 2).astype(BF16)

    zk = _dot(kvb, wk_ref[...])
    for hh in range(N_HEADS_B):
        kb_ref[:, hh * LANES:(hh + 1) * LANES] = (zk[:, hh * LANES:(hh + 1) * LANES] + kr).astype(BF16)
    vb_ref[...] = (_dot(kvb, wv_ref[...]) + _ones_lane(N_HEADS_B * LANES)).astype(BF16)

    ga_ref[...] = jax.nn.sigmoid(_dot(hb, win_ref[:, _C_GA:_C_GB])).astype(BF16)
    gb_ref[...] = jax.nn.sigmoid(_dot(hb, win_ref[:, _C_GB:_C_END])).astype(BF16)


def _inproj(x, mod_l, n1, win, gqa, gka, gql, gkv, wq, wk, wv, ta, tb, tm, moe=None):
    bsz, seq, d = x.shape
    nt = seq // tm
    row = lambda w: pl.BlockSpec((None, tm, w), lambda b, i: (b, i, 0))
    full = _resident
    widths = (N_HEADS_A * LANES, N_KV_A * LANES, N_KV_A * LANES,
              N_HEADS_B * LANES, N_HEADS_B * LANES, N_HEADS_B * LANES, D_MODEL, D_MODEL)
    mod_spec = pl.BlockSpec((None, N_MOD, d), lambda b, i: (b, 0, 0))
    body_specs = [
        mod_spec,
        full(n1), full(win), full(gqa), full(gka), full(gql), full(gkv),
        full(wq), full(wk), full(wv),
        pl.BlockSpec((3, tm, LANES), lambda b, i: (0, i, 0)),
        pl.BlockSpec((6, tm, LANES), lambda b, i: (0, i, 0)),
    ]
    body_args = (mod_l, n1, win, gqa, gka, gql, gkv, wq, wk, wv, ta, tb)
    out_shape = [jax.ShapeDtypeStruct((bsz, seq, w), BF16) for w in widths]
    out_specs = [row(w) for w in widths]
    if moe is None:
        return pl.pallas_call(
            _inproj_kernel,
            out_shape=out_shape,
            grid=(bsz, nt),
            in_specs=[row(d)] + body_specs,
            out_specs=out_specs,
            compiler_params=_cparams(("parallel", "parallel")),
            name="inproj",
        )(x, *body_args)
    ys, dest, mod_prev = moe
    dest_t = dest.reshape(bsz * nt, 1, tm)
    n_tiles = bsz * nt
    return pl.pallas_call(
        _inproj_combine_kernel,
        out_shape=out_shape + [jax.ShapeDtypeStruct((bsz, seq, d), F32)],
        grid=(bsz, nt),
        in_specs=[
            pl.BlockSpec((None, 1, tm), lambda b, i: (b * nt + i, 0, 0), memory_space=pltpu.SMEM),
            pl.BlockSpec((None, 1, tm), lambda b, i: (jnp.minimum(b * nt + i + 1, n_tiles - 1), 0, 0),
                         memory_space=pltpu.SMEM),
            row(d), mod_spec,
            pl.BlockSpec(memory_space=pl.ANY),
        ] + body_specs,
        out_specs=out_specs + [row(d)],
        scratch_shapes=[pltpu.VMEM((2, tm, d), F32), pltpu.SemaphoreType.DMA((2,))],
        compiler_params=_cparams(("arbitrary", "arbitrary")),
        name="combine_inproj",
    )(dest_t, dest_t, x, mod_prev, ys, *body_args)


def _attn_kernel(q_ref, k_ref, v_ref, o_ref, *, group, tk, unroll):
    tq = q_ref.shape[0]
    seq = k_ref.shape[0]
    if group == 1:
        q = q_ref[...]
    else:
        q = jnp.concatenate(
            [q_ref[:, g * LANES:(g + 1) * LANES] for g in range(group)], axis=0)
    m_rows = group * tq

    def body(j, carry):
        m, acc = carry
        start = pl.multiple_of(j * tk, tk)
        k = k_ref[pl.ds(start, tk), :]
        v = v_ref[pl.ds(start, tk), :]
        s = lax.dot_general(q, k, (((1,), (1,)), ((), ())), preferred_element_type=F32)
        m_new = jnp.maximum(m, jnp.max(s, axis=-1, keepdims=True))
        alpha = jnp.exp2(m - m_new)
        p = jnp.exp2((s - m_new).astype(BF16))
        return m_new, alpha * acc + _dot(p, v)

    n_blk = seq // tk
    init = (jnp.full((m_rows, 1), -jnp.inf, F32), jnp.zeros((m_rows, LANES), F32))
    _, acc = lax.fori_loop(0, n_blk, body, init, unroll=unroll if n_blk % unroll == 0 else 1)
    o = (acc / acc[:, ONES_LANE:ONES_LANE + 1]).astype(o_ref.dtype)
    for g in range(group):
        o_ref[:, g * LANES:(g + 1) * LANES] = o[g * tq:(g + 1) * tq, :]


def _attention(q, k, v, *, group, tq, tk, unroll):
    bsz, seq, qw = q.shape
    n_kv = k.shape[-1] // LANES
    return pl.pallas_call(
        functools.partial(_attn_kernel, group=group, tk=tk, unroll=unroll),
        out_shape=jax.ShapeDtypeStruct((bsz, seq, qw), BF16),
        grid=(bsz, n_kv, seq // tq),
        in_specs=[
            pl.BlockSpec((None, tq, group * LANES), lambda b, h, i: (b, i, h)),
            pl.BlockSpec((None, seq, LANES), lambda b, h, i: (b, 0, h)),
            pl.BlockSpec((None, seq, LANES), lambda b, h, i: (b, 0, h)),
        ],
        out_specs=pl.BlockSpec((None, tq, group * LANES), lambda b, h, i: (b, i, h)),
        compiler_params=_cparams(("parallel", "parallel", "parallel")),
        name="flash_attn_g%d" % group,
    )(q, k, v)


def _first_argmax(vals):
    best, idx = vals[0], jnp.zeros_like(vals[0], dtype=jnp.int32)
    for i in range(1, len(vals)):
        gt = vals[i] > best
        best = jnp.where(gt, vals[i], best)
        idx = jnp.where(gt, i, idx)
    return best, idx


def _pick(vals, idx):
    out = vals[0]
    for i in range(1, len(vals)):
        out = jnp.where(idx == i, vals[i], out)
    return out


def _outproj_kernel(oa_ref, ob_ref, ga_ref, gb_ref, x_ref, mod_ref, n2_ref,
                    wpa_ref, wpb_ref, wo_ref, wr_ref, rb_ref,
                    x1_ref, h2_ref, bkt_ref, rank_ref, cnt_ref, base_ref, gwt_ref, upper_ref):
    tm = x_ref.shape[0]

    @pl.when((pl.program_id(0) == 0) & (pl.program_id(1) == 0))
    def _():
        base_ref[...] = jnp.zeros_like(base_ref)
        r_i = lax.broadcasted_iota(jnp.int32, (tm, tm), 0)
        c_i = lax.broadcasted_iota(jnp.int32, (tm, tm), 1)
        upper_ref[...] = jnp.where(r_i < c_i, 1.0, 0.0).astype(BF16)

    ya = _dot(oa_ref[...], wpa_ref[...])
    yb = _dot(ob_ref[...], wpb_ref[...])
    merged = ga_ref[...].astype(F32) * ya + gb_ref[...].astype(F32) * yb
    att = _dot(merged.astype(BF16), wo_ref[...])
    g1 = mod_ref[2:3, :]
    sh2 = mod_ref[3:4, :]
    sc2 = mod_ref[4:5, :]
    x1 = x_ref[...] + g1 * att
    x1_ref[...] = x1
    h2 = (_rms(x1, D_MODEL) * n2_ref[...]) * (1.0 + sc2) + sh2
    hi_bits = lax.bitcast_convert_type(h2[:, 0:XS_HALF].astype(BF16).astype(F32), U32)
    lo_bits = lax.bitcast_convert_type(h2[:, XS_HALF:].astype(BF16).astype(F32), U32)
    h2_ref[:, 0:XS_HALF] = hi_bits | (lo_bits >> 16)

    h2_hi, h2_lo = _split(h2)
    both = _dot(h2_hi, wr_ref[...])
    logits = both[:, 0:LANES] + (both[:, LANES:] + _dot(h2_lo, wr_ref[:, 0:LANES]))
    lt = logits.T[0:N_EXPERTS, :]
    scores = jax.nn.sigmoid(lt)
    biased = scores + rb_ref[...]
    brow = [biased[e:e + 1, :] for e in range(N_EXPERTS)]
    srow = [scores[e:e + 1, :] for e in range(N_EXPERTS)]
    grp = []
    for g in range(N_GROUPS):
        a, b, c, d = brow[4 * g:4 * g + 4]
        hi1, lo1 = jnp.maximum(a, b), jnp.minimum(a, b)
        hi2, lo2 = jnp.maximum(c, d), jnp.minimum(c, d)
        grp.append(jnp.maximum(hi1, hi2) + jnp.maximum(jnp.minimum(hi1, hi2),
                                                       jnp.maximum(lo1, lo2)))
    _, gsel = _first_argmax(grp)
    bsel = [_pick([brow[4 * g + i] for g in range(N_GROUPS)], gsel)
            for i in range(EXPERTS_PER_GROUP)]
    ssel = [_pick([srow[4 * g + i] for g in range(N_GROUPS)], gsel)
            for i in range(EXPERTS_PER_GROUP)]
    _, i0 = _first_argmax(bsel)
    _, i1 = _first_argmax([jnp.where(i0 == i, -jnp.inf, bsel[i])
                           for i in range(EXPERTS_PER_GROUP)])
    w0 = _pick(ssel, i0)
    w1 = _pick(ssel, i1)
    wsum = w0 + w1
    first_lo = i0 < i1
    lo = jnp.where(first_lo, i0, i1)
    hi = jnp.where(first_lo, i1, i0)
    w_lo = jnp.where(first_lo, w0, w1) / wsum
    w_hi = jnp.where(first_lo, w1, w0) / wsum
    pair = jnp.where(lo == 0, 0, jnp.where(lo == 1, 3, 5)) + (hi - lo - 1)
    bucket = gsel * N_PAIRS + pair
    bkt_ref[...] = bucket

    gwt_ref[...] = jnp.zeros_like(gwt_ref)
    gwt_ref[0:1, :] = w_lo
    gwt_ref[1:2, :] = w_hi
    h2_ref[:, XS_HALF:] = lax.bitcast_convert_type(gwt_ref[...].T, U32)

    onehot = lax.broadcasted_iota(jnp.int32, (N_BUCKETS, tm), 0) == bucket
    cnt = jnp.where(onehot, 1.0, 0.0)
    pref = _dot(cnt.astype(BF16), upper_ref[...]) + base_ref[:, 0:1]
    rank_ref[...] = jnp.sum(jnp.where(onehot, pref, 0.0), axis=0, keepdims=True).astype(jnp.int32)
    base_ref[...] = base_ref[...] + jnp.sum(cnt, axis=1, keepdims=True)
    cnt_ref[...] = base_ref[...]


def _outproj(oa, ob, ga, gb, x, mod_l, n2, wpa, wpb, wo, wr, rb, tm):
    bsz, seq, d = x.shape
    nt = seq // tm
    row = lambda w: pl.BlockSpec((None, tm, w), lambda b, i: (b, i, 0))
    full = _resident
    tok = pl.BlockSpec((1, tm), lambda b, i: (0, b * nt + i))
    t = bsz * seq
    return pl.pallas_call(
        _outproj_kernel,
        out_shape=[
            jax.ShapeDtypeStruct((bsz, seq, d), F32),
            jax.ShapeDtypeStruct((bsz, seq, XS_COLS), U32),
            jax.ShapeDtypeStruct((1, t), jnp.int32),
            jax.ShapeDtypeStruct((1, t), jnp.int32),
            jax.ShapeDtypeStruct((N_BUCKETS, LANES), F32),
        ],
        grid=(bsz, nt),
        in_specs=[
            row(d), row(d), row(d), row(d), row(d),
            pl.BlockSpec((None, N_MOD, d), lambda b, i: (b, 0, 0)),
            full(n2), full(wpa), full(wpb), full(wo), full(wr), full(rb),
        ],
        out_specs=[row(d), row(XS_COLS), tok, tok,
                   pl.BlockSpec((N_BUCKETS, LANES), lambda b, i: (0, 0))],
        scratch_shapes=[pltpu.VMEM((N_BUCKETS, LANES), F32), pltpu.VMEM((LANES, tm), F32),
                        pltpu.VMEM((tm, tm), BF16)],
        compiler_params=_cparams(("arbitrary", "arbitrary")),
        name="outproj_router",
    )(oa, ob, ga, gb, x, mod_l, n2, wpa, wpb, wo, wr, rb)


def _dispatch_kernel(dest_ref, h_ref, xs_ref, sem):
    tm = h_ref.shape[0]

    def issue(t, c):
        pltpu.make_async_copy(h_ref.at[pl.ds(t, 1)], xs_ref.at[pl.ds(dest_ref[0, t], 1)],
                              sem).start()
        return c

    lax.fori_loop(0, tm, issue, 0, unroll=DMA_UNROLL)
    pltpu.make_async_copy(h_ref, xs_ref.at[pl.ds(0, tm)], sem).wait()


def _dispatch(h2, dest, n_rows, tm):
    t, d = h2.shape
    nt = t // tm
    return pl.pallas_call(
        _dispatch_kernel,
        out_shape=jax.ShapeDtypeStruct((n_rows, d), h2.dtype),
        grid=(nt,),
        in_specs=[
            pl.BlockSpec((None, 1, tm), lambda i: (i, 0, 0), memory_space=pltpu.SMEM),
            pl.BlockSpec((tm, d), lambda i: (i, 0)),
        ],
        out_specs=pl.BlockSpec(memory_space=pl.ANY),
        scratch_shapes=[pltpu.SemaphoreType.DMA],
        compiler_params=_cparams(("arbitrary",)),
        name="moe_dispatch",
    )(dest.reshape(nt, 1, tm), h2)


def _expert_kernel(ea_ref, eb_ref, br_ref, nv_ref, x_ref, w1a_ref, w3a_ref, w2a_ref,
                   w1b_ref, w3b_ref, w2b_ref, y_ref):
    b = pl.program_id(0)

    @pl.when(b < nv_ref[0])
    def _():
        bm = x_ref.shape[0]
        live = lax.broadcasted_iota(jnp.int32, (bm, 1), 0) < br_ref[b]
        words = jnp.where(live, x_ref[:, 0:XS_HALF], 0)
        x = jnp.concatenate(
            [lax.bitcast_convert_type((words >> 16) << 16, F32).astype(BF16),
             lax.bitcast_convert_type(words << 16, F32).astype(BF16)], axis=1)
        gates = lax.bitcast_convert_type(jnp.where(live, x_ref[:, XS_HALF:], 0), F32)

        def hidden(w1_ref, w3_ref, gate):
            a = _dot(x, w1_ref[...])
            g = _dot(x, w3_ref[...])
            return ((a * jax.nn.sigmoid(a)) * g * gate).astype(BF16)

        h_lo = hidden(w1a_ref, w3a_ref, gates[:, 0:1])
        h_hi = hidden(w1b_ref, w3b_ref, gates[:, 1:2])
        y_ref[...] = _dot(h_lo, w2a_ref[...]) + _dot(h_hi, w2b_ref[...])


def _experts(xs, w1, w3, w2, layer, blk_ea, blk_eb, blk_rows, n_valid, bm):
    n_rows = xs.shape[0]
    nb = n_rows // bm
    d, f = w1.shape[-2:]
    last = lambda b, nv: jnp.minimum(b, nv[0] - 1)
    xmap = lambda b, ea, eb, br, nv: (last(b, nv), 0)
    amap = lambda b, ea, eb, br, nv: (layer, ea[last(b, nv)], 0, 0)
    bmap = lambda b, ea, eb, br, nv: (layer, eb[last(b, nv)], 0, 0)
    return pl.pallas_call(
        _expert_kernel,
        out_shape=jax.ShapeDtypeStruct((n_rows, d), F32),
        grid_spec=pltpu.PrefetchScalarGridSpec(
            num_scalar_prefetch=4,
            grid=(nb,),
            in_specs=[
                pl.BlockSpec((bm, XS_COLS), xmap),
                pl.BlockSpec((None, None, d, f), amap),
                pl.BlockSpec((None, None, d, f), amap),
                pl.BlockSpec((None, None, f, d), amap),
                pl.BlockSpec((None, None, d, f), bmap),
                pl.BlockSpec((None, None, d, f), bmap),
                pl.BlockSpec((None, None, f, d), bmap),
            ],
            out_specs=pl.BlockSpec((bm, d), xmap),
        ),
        compiler_params=_cparams(("arbitrary",)),
        name="moe_experts",
    )(blk_ea, blk_eb, blk_rows, n_valid, xs, w1, w3, w2, w1, w3, w2)


def _combine_kernel(dest_ref, x1_ref, mod_ref, nf_ref, ys_ref, o_ref, buf, sem):
    tm = x1_ref.shape[0]

    def issue(t, c):
        pltpu.make_async_copy(ys_ref.at[pl.ds(dest_ref[0, t], 1)], buf.at[pl.ds(t, 1)],
                              sem).start()
        return c

    lax.fori_loop(0, tm, issue, 0, unroll=DMA_UNROLL)
    pltpu.make_async_copy(ys_ref.at[pl.ds(0, tm)], buf, sem).wait()
    x2 = x1_ref[...] + mod_ref[5:6, :] * buf[...]
    o_ref[...] = _rms(x2, D_MODEL) * nf_ref[...]


def _combine(ys, dest, x1, mod_l, nf, tm):
    bsz, seq, d = x1.shape
    nt = seq // tm
    return pl.pallas_call(
        _combine_kernel,
        out_shape=jax.ShapeDtypeStruct((bsz, seq, d), F32),
        grid=(bsz, nt),
        in_specs=[
            pl.BlockSpec((None, 1, tm), lambda b, i: (b * nt + i, 0, 0),
                         memory_space=pltpu.SMEM),
            pl.BlockSpec((None, tm, d), lambda b, i: (b, i, 0)),
            pl.BlockSpec((None, N_MOD, d), lambda b, i: (b, 0, 0)),
            pl.BlockSpec((1, d), lambda b, i: (0, 0)),
            pl.BlockSpec(memory_space=pl.ANY),
        ],
        out_specs=pl.BlockSpec((None, tm, d), lambda b, i: (b, i, 0)),
        scratch_shapes=[pltpu.VMEM((tm, d), F32), pltpu.SemaphoreType.DMA],
        compiler_params=_cparams(("arbitrary", "arbitrary")),
        name="moe_combine",
    )(dest.reshape(bsz * nt, 1, tm), x1, mod_l, nf, ys)


def _rope_tables(seq):
    rows = seq // GRID_W
    row = jnp.broadcast_to(jnp.arange(rows, dtype=F32)[:, None], (rows, GRID_W)).reshape(seq)
    col = jnp.broadcast_to(jnp.arange(GRID_W, dtype=F32)[None, :], (rows, GRID_W)).reshape(seq)

    def cs(rot_dim):
        n = rot_dim // 4
        inv = 1.0 / (ROPE_THETA ** (jnp.arange(n, dtype=F32) / n))
        ang = jnp.concatenate([row[:, None] * inv, col[:, None] * inv], axis=-1)
        return jnp.cos(ang), jnp.sin(ang)

    def place(pieces):
        out = jnp.zeros((seq, LANES), F32)
        for off, val in pieces:
            out = out.at[:, off:off + val.shape[1]].set(val)
        return out

    ca, sa = cs(HEAD_DIM_A)
    ha = HEAD_DIM_A // 2
    scale_a = HEAD_DIM_A ** -0.5 * LOG2E
    ta = jnp.stack([place([(0, ca), (ha, ca)]), place([(0, -sa)]), place([(ha, sa)])]) * scale_a
    cb, sb = cs(QK_ROPE)
    hb = QK_ROPE // 2
    scale_b = (QK_NOPE + QK_ROPE) ** -0.5 * LOG2E
    ones = jnp.ones((seq, QK_NOPE), F32)
    cos_q = place([(0, ones), (QK_NOPE, cb), (QK_NOPE + hb, cb)]) * scale_b
    sin_lo = place([(QK_NOPE, -sb)])
    sin_hi = place([(QK_NOPE + hb, sb)])
    cos_k = place([(QK_NOPE, cb), (QK_NOPE + hb, cb)])
    tb = jnp.stack([cos_q, sin_lo * scale_b, sin_hi * scale_b, cos_k, sin_lo, sin_hi])
    return ta, tb


def _pad_heads(w, n_heads, width, offset=0):
    k = w.shape[0]
    w = w.reshape(k, n_heads, width)
    out = jnp.zeros((k, n_heads, LANES), w.dtype).at[:, :, offset:offset + width].set(w)
    return out.reshape(k, n_heads * LANES)


def _prep_layer(w_in, w_q_b, w_kv_b, w_proj_a, w_proj_b):
    o = 0
    parts = []
    sizes = (N_HEADS_A * HEAD_DIM_A, N_KV_A * HEAD_DIM_A, N_KV_A * HEAD_DIM_A,
             Q_LORA, KV_LORA, QK_ROPE, D_MODEL, D_MODEL)
    segs = []
    for s in sizes:
        segs.append(w_in[:, o:o + s])
        o += s
    parts = [
        segs[0],
        _pad_heads(segs[1], N_KV_A, HEAD_DIM_A),
        _pad_heads(segs[2], N_KV_A, HEAD_DIM_A),
        segs[3], segs[4],
        _pad_heads(segs[5], 1, QK_ROPE, offset=QK_NOPE),
        segs[6], segs[7],
    ]
    win = jnp.concatenate(parts, axis=1).astype(BF16)
    wq = _pad_heads(w_q_b, N_HEADS_B, QK_NOPE + QK_ROPE).astype(BF16)
    wkv = w_kv_b.reshape(KV_LORA, N_HEADS_B, QK_NOPE + V_DIM_B)
    wk = _pad_heads(wkv[:, :, :QK_NOPE].reshape(KV_LORA, -1), N_HEADS_B, QK_NOPE).astype(BF16)
    wv = _pad_heads(wkv[:, :, QK_NOPE:].reshape(KV_LORA, -1), N_HEADS_B, V_DIM_B).astype(BF16)

    def pad_rows(w, n_heads, width):
        return _pad_heads(w.T, n_heads, width).T.astype(BF16)

    wpa = pad_rows(w_proj_a, N_HEADS_A, HEAD_DIM_A)
    wpb = pad_rows(w_proj_b, N_HEADS_B, V_DIM_B)
    return win, wq, wk, wv, wpa, wpb


def _pad_lanes(g, offset=0):
    return jnp.zeros((1, LANES), F32).at[0, offset:offset + g.shape[0]].set(g)


def kernel(x, c, w_ada, b_ada, norm1, w_in, q_norm_a, k_norm_a, q_a_norm, kv_norm, w_q_b, w_kv_b, w_proj_a, w_proj_b, w_o, norm2, w_router, router_bias, w1, w3, w2, norm_f):
    bsz, seq, d = x.shape
    depth = w_in.shape[0]
    t = bsz * seq
    tm = min(256, seq)
    tm_proj = min(512, seq)
    bm = min(256, t)
    tk = min(512, seq)

    mod = _modulation(c, w_ada, b_ada).reshape(depth, bsz, N_MOD, d)
    ta, tb = _rope_tables(seq)
    wr_pad = jnp.zeros((d, LANES), F32).at[:, :N_EXPERTS].set(w_router)
    wr = jnp.concatenate(_split(wr_pad), axis=1)
    rb = router_bias.reshape(N_EXPERTS, 1)
    n_rows = ((t + bm - 1) // bm) * bm + N_BUCKETS * bm
    nb = n_rows // bm
    w1b, w3b, w2b = w1.astype(BF16), w3.astype(BF16), w2.astype(BF16)
    nf = norm_f.reshape(1, d)
    bucket_ids = jnp.arange(N_BUCKETS, dtype=jnp.int32)
    bucket_ea = (bucket_ids // N_PAIRS) * EXPERTS_PER_GROUP + jnp.array(PAIR_LO, jnp.int32)[bucket_ids % N_PAIRS]
    bucket_eb = (bucket_ids // N_PAIRS) * EXPERTS_PER_GROUP + jnp.array(PAIR_HI, jnp.int32)[bucket_ids % N_PAIRS]

    moe = None
    for l in range(depth):
        win, wq, wk, wv, wpa, wpb = _prep_layer(w_in[l], w_q_b[l], w_kv_b[l],
                                                w_proj_a[l], w_proj_b[l])
        outs = _inproj(
            x, mod[l], norm1[l].reshape(1, d), win,
            _pad_lanes(q_norm_a[l]), _pad_lanes(k_norm_a[l]),
            q_a_norm[l].reshape(1, Q_LORA), kv_norm[l].reshape(1, KV_LORA),
            wq, wk, wv, ta, tb, tm_proj, moe=moe)
        qa, ka, va, qb, kb, vb, ga, gb = outs[:8]
        if moe is not None:
            x = outs[8]
        n_unroll = min(8, seq // tk)
        oa = _attention(qa, ka, va, group=GROUP_A, tq=min(256, seq), tk=tk, unroll=n_unroll)
        ob = _attention(qb, kb, vb, group=1, tq=min(1024, seq), tk=tk, unroll=n_unroll)
        x1, h2, bucket, rank, cnt = _outproj(
            oa, ob, ga, gb, x, mod[l], norm2[l].reshape(1, d), wpa, wpb,
            w_o[l].astype(BF16), wr, rb, tm)

        counts = cnt[:, 0].astype(jnp.int32)
        padded = ((counts + bm - 1) // bm) * bm
        pends = jnp.cumsum(padded)
        pstarts = pends - padded
        dest = rank
        for e in range(N_BUCKETS):
            dest = dest + jnp.where(bucket == e, pstarts[e], 0)
        blk_start = jnp.arange(nb, dtype=jnp.int32) * bm
        blk_b = jnp.minimum(jnp.sum(blk_start[:, None] >= pends[None, :], axis=1),
                            N_BUCKETS - 1).astype(jnp.int32)
        blk_rows = jnp.clip(pstarts[blk_b] + counts[blk_b] - blk_start, 0, bm).astype(jnp.int32)
        n_valid = jnp.maximum(pends[-1:] // bm, 1).astype(jnp.int32)

        xs = _dispatch(h2.reshape(t, XS_COLS), dest, n_rows, tm)
        ys = _experts(xs, w1b, w3b, w2b, l, bucket_ea[blk_b], bucket_eb[blk_b], blk_rows,
                      n_valid, bm)
        x, moe = x1, (ys, dest, mod[l])
    return _combine(ys, dest, x1, mod[depth - 1], nf, tm)
```

```python
import functools

import jax
import jax.numpy as jnp
from jax import lax
from jax.experimental import pallas as pl
from jax.experimental.pallas import tpu as pltpu

F32 = jnp.float32
BF16 = jnp.bfloat16

D_MODEL = 1024
GRID_W = 64
ROPE_THETA = 10000.0
EPS = 1e-6
N_HEADS_A = 8
N_KV_A = 2
GROUP_A = N_HEADS_A // N_KV_A
HEAD_DIM_A = 64
N_HEADS_B = 8
Q_LORA = 384
KV_LORA = 256
QK_NOPE = 64
QK_ROPE = 32
V_DIM_B = 64
N_MOD = 6
N_EXPERTS = 16
N_GROUPS = 4
EXPERTS_PER_GROUP = N_EXPERTS // N_GROUPS
D_EXPERT = 512
N_PAIRS = EXPERTS_PER_GROUP * (EXPERTS_PER_GROUP - 1) // 2
N_BUCKETS = N_GROUPS * N_PAIRS
PAIR_LO = (0, 0, 0, 1, 1, 2)
PAIR_HI = (1, 2, 3, 2, 3, 3)

LANES = 128
ONES_LANE = 64
LOG2E = 1.4426950408889634
U32 = jnp.uint32
XS_HALF = D_MODEL // 2
XS_COLS = XS_HALF + LANES
DMA_UNROLL = 8
VMEM_LIMIT = 56 * 1024 * 1024

_C_QA = 0
_C_KA = _C_QA + N_HEADS_A * HEAD_DIM_A
_C_VA = _C_KA + N_KV_A * LANES
_C_QL = _C_VA + N_KV_A * LANES
_C_KVL = _C_QL + Q_LORA
_C_KR = _C_KVL + KV_LORA
_C_GA = _C_KR + LANES
_C_GB = _C_GA + D_MODEL
_C_END = _C_GB + D_MODEL


def _cparams(sem):
    return pltpu.CompilerParams(dimension_semantics=sem, vmem_limit_bytes=VMEM_LIMIT)


def _resident(a):
    return pl.BlockSpec(a.shape, lambda *_: (0,) * a.ndim, pipeline_mode=pl.Buffered(1))


def _split(a):
    hi = a.astype(BF16)
    lo = (a - hi.astype(F32)).astype(BF16)
    return hi, lo


def _dot(a, b):
    return jnp.dot(a, b, preferred_element_type=F32)


def _dot3(a, b):
    ah, al = _split(a)
    bh, bl = _split(b)
    return _dot(ah, bh) + (_dot(ah, bl) + _dot(al, bh))


def _mod_kernel(c_ref, w_ref, b_ref, o_ref):
    c = c_ref[...]
    cond = c * jax.nn.sigmoid(c)
    o_ref[...] = _dot3(cond, w_ref[...]) + b_ref[...]


def _modulation(c, w_ada, b_ada):
    depth, d, n = w_ada.shape
    bsz = c.shape[0]
    cb = 1536
    return pl.pallas_call(
        _mod_kernel,
        out_shape=jax.ShapeDtypeStruct((depth, bsz, n), F32),
        grid=(depth, n // cb),
        in_specs=[
            pl.BlockSpec((bsz, d), lambda l, j: (0, 0)),
            pl.BlockSpec((None, d, cb), lambda l, j: (l, 0, j)),
            pl.BlockSpec((None, 1, cb), lambda l, j: (l, 0, j)),
        ],
        out_specs=pl.BlockSpec((None, bsz, cb), lambda l, j: (l, 0, j)),
        compiler_params=_cparams(("parallel", "parallel")),
        name="adaln_mod",
    )(c, w_ada, b_ada.reshape(depth, 1, n))


def _rms(v, n):
    return v * lax.rsqrt(jnp.sum(v * v, axis=-1, keepdims=True) * (1.0 / n) + EPS)


def _rope(v, cos, sin_lo, sin_hi, shift):
    return (v * cos + pltpu.roll(v, LANES - shift, 1) * sin_lo
            + pltpu.roll(v, shift, 1) * sin_hi)


def _ones_lane(width):
    lane = lax.broadcasted_iota(jnp.int32, (1, width), 1)
    return jnp.where(lane % LANES == ONES_LANE, 1.0, 0.0)


def _inproj_kernel(x_ref, *refs):
    _inproj_body(x_ref[...], *refs)


def _inproj_combine_kernel(dcur_ref, dnext_ref, x1_ref, modp_ref, ys_ref, *refs):
    body_refs, (xnew_ref, buf, sems) = refs[:-3], refs[-3:]
    tm = x1_ref.shape[0]
    step = pl.program_id(0) * pl.num_programs(1) + pl.program_id(1)
    last = pl.num_programs(0) * pl.num_programs(1) - 1
    slot = step % 2

    def row_copy(dest_ref, t, s):
        return pltpu.make_async_copy(ys_ref.at[pl.ds(dest_ref[0, t], 1)],
                                     buf.at[s, pl.ds(t, 1)], sems.at[s])

    def tile_wait(s):
        pltpu.make_async_copy(ys_ref.at[pl.ds(0, tm)], buf.at[s], sems.at[s]).wait()

    @pl.when(step == 0)
    def _():
        def issue(t, c):
            row_copy(dcur_ref, t, 0).start()
            return c
        lax.fori_loop(0, tm, issue, 0, unroll=DMA_UNROLL)

    tile_wait(slot)
    x = x1_ref[...] + modp_ref[5:6, :] * buf[slot]
    xnew_ref[...] = x
    for t in range(tm):
        pltpu.async_copy(ys_ref.at[pl.ds(dnext_ref[0, t], 1)], buf.at[1 - slot, pl.ds(t, 1)],
                         sems.at[1 - slot], priority=t % 2)
    _inproj_body(x, *body_refs)

    @pl.when(step == last)
    def _():
        tile_wait(1 - slot)


def _inproj_body(x, mod_ref, n1_ref, win_ref, gqa_ref, gka_ref, gql_ref, gkv_ref,
                 wq_ref, wk_ref, wv_ref, ta_ref, tb_ref,
                 qa_ref, ka_ref, va_ref, qb_ref, kb_ref, vb_ref, ga_ref, gb_ref):
    sh1 = mod_ref[0:1, :]
    sc1 = mod_ref[1:2, :]
    h = (_rms(x, D_MODEL) * n1_ref[...]) * (1.0 + sc1) + sh1
    hb = h.astype(BF16)

    cos_a, sin_a_lo, sin_a_hi = ta_ref[0], ta_ref[1], ta_ref[2]
    cos_b, sin_b_lo, sin_b_hi = tb_ref[0], tb_ref[1], tb_ref[2]
    cos_k, sin_k_lo, sin_k_hi = tb_ref[3], tb_ref[4], tb_ref[5]

    gqa = gqa_ref[...]
    z = _dot(hb, win_ref[:, _C_QA:_C_KA])
    low_half = lax.broadcasted_iota(jnp.int32, (1, LANES), 1) < HEAD_DIM_A
    for hh in range(N_HEADS_A):
        pair = z[:, (hh // 2) * LANES:(hh // 2 + 1) * LANES]
        if hh % 2:
            pair = pltpu.roll(pair, HEAD_DIM_A, 1)
        zn = _rms(jnp.where(low_half, pair, 0.0), HEAD_DIM_A) * gqa
        qa_ref[:, hh * LANES:(hh + 1) * LANES] = _rope(
            zn, cos_a, sin_a_lo, sin_a_hi, HEAD_DIM_A // 2).astype(BF16)
    gka = gka_ref[...]
    z = _dot(hb, win_ref[:, _C_KA:_C_QL])
    for hh in range(N_KV_A):
        zn = _rms(z[:, hh * LANES:(hh + 1) * LANES], HEAD_DIM_A) * gka
        ka_ref[:, hh * LANES:(hh + 1) * LANES] = _rope(
            zn, cos_a, sin_a_lo, sin_a_hi, HEAD_DIM_A // 2).astype(BF16)
    va_ref[...] = (z[:, N_KV_A * LANES:] + _ones_lane(N_KV_A * LANES)).astype(BF16)

    z = _dot(hb, win_ref[:, _C_QL:_C_GA])
    qlb = (_rms(z[:, 0:Q_LORA], Q_LORA) * gql_ref[...]).astype(BF16)
    kvb = (_rms(z[:, Q_LORA:Q_LORA + KV_LORA], KV_LORA) * gkv_ref[...]).astype(BF16)
    kr = _rope(z[:, Q_LORA + KV_LORA:], cos_k, sin_k_lo, sin_k_hi, QK_ROPE // 2)
    zq = _dot(qlb, wq_ref[...])
    for hh in range(N_HEADS_B):
        qb_ref[:, hh * LANES:(hh + 1) * LANES] = _rope(
            zq[:, hh * LANES:(hh + 1) * LANES], cos_b, sin_b_lo, sin_b_hi,
            QK_ROPE // 2).astype(BF16)

    zk = _dot(kvb, wk_ref[...])
    for hh in range(N_HEADS_B):
        kb_ref[:, hh * LANES:(hh + 1) * LANES] = (zk[:, hh * LANES:(hh + 1) * LANES] + kr).astype(BF16)
    vb_ref[...] = (_dot(kvb, wv_ref[...]) + _ones_lane(N_HEADS_B * LANES)).astype(BF16)

    ga_ref[...] = jax.nn.sigmoid(_dot(hb, win_ref[:, _C_GA:_C_GB])).astype(BF16)
    gb_ref[...] = jax.nn.sigmoid(_dot(hb, win_ref[:, _C_GB:_C_END])).astype(BF16)


def _inproj(x, mod_l, n1, win, gqa, gka, gql, gkv, wq, wk, wv, ta, tb, tm, moe=None):
    bsz, seq, d = x.shape
    nt = seq // tm
    row = lambda w: pl.BlockSpec((None, tm, w), lambda b, i: (b, i, 0))
    full = _resident
    widths = (N_HEADS_A * LANES, N_KV_A * LANES, N_KV_A * LANES,
              N_HEADS_B * LANES, N_HEADS_B * LANES, N_HEADS_B * LANES, D_MODEL, D_MODEL)
    mod_spec = pl.BlockSpec((None, N_MOD, d), lambda b, i: (b, 0, 0))
    body_specs = [
        mod_spec,
        full(n1), full(win), full(gqa), full(gka), full(gql), full(gkv),
        full(wq), full(wk), full(wv),
        pl.BlockSpec((3, tm, LANES), lambda b, i: (0, i, 0)),
        pl.BlockSpec((6, tm, LANES), lambda b, i: (0, i, 0)),
    ]
    body_args = (mod_l, n1, win, gqa, gka, gql, gkv, wq, wk, wv, ta, tb)
    out_shape = [jax.ShapeDtypeStruct((bsz, seq, w), BF16) for w in widths]
    out_specs = [row(w) for w in widths]
    if moe is None:
        return pl.pallas_call(
            _inproj_kernel,
            out_shape=out_shape,
            grid=(bsz, nt),
            in_specs=[row(d)] + body_specs,
            out_specs=out_specs,
            compiler_params=_cparams(("parallel", "parallel")),
            name="inproj",
        )(x, *body_args)
    ys, dest, mod_prev = moe
    dest_t = dest.reshape(bsz * nt, 1, tm)
    n_tiles = bsz * nt
    return pl.pallas_call(
        _inproj_combine_kernel,
        out_shape=out_shape + [jax.ShapeDtypeStruct((bsz, seq, d), F32)],
        grid=(bsz, nt),
        in_specs=[
            pl.BlockSpec((None, 1, tm), lambda b, i: (b * nt + i, 0, 0), memory_space=pltpu.SMEM),
            pl.BlockSpec((None, 1, tm), lambda b, i: (jnp.minimum(b * nt + i + 1, n_tiles - 1), 0, 0),
                         memory_space=pltpu.SMEM),
            row(d), mod_spec,
            pl.BlockSpec(memory_space=pl.ANY),
        ] + body_specs,
        out_specs=out_specs + [row(d)],
        scratch_shapes=[pltpu.VMEM((2, tm, d), F32), pltpu.SemaphoreType.DMA((2,))],
        compiler_params=_cparams(("arbitrary", "arbitrary")),
        name="combine_inproj",
    )(dest_t, dest_t, x, mod_prev, ys, *body_args)


def _attn_kernel(q_ref, k_ref, v_ref, o_ref, *, group, tk, unroll):
    tq = q_ref.shape[0]
    seq = k_ref.shape[0]
    if group == 1:
        q = q_ref[...]
    else:
        q = jnp.concatenate(
            [q_ref[:, g * LANES:(g + 1) * LANES] for g in range(group)], axis=0)
    m_rows = group * tq

    def body(j, carry):
        m, acc = carry
        start = pl.multiple_of(j * tk, tk)
        k = k_ref[pl.ds(start, tk), :]
        v = v_ref[pl.ds(start, tk), :]
        s = lax.dot_general(q, k, (((1,), (1,)), ((), ())), preferred_element_type=F32)
        m_new = jnp.maximum(m, jnp.max(s, axis=-1, keepdims=True))
        alpha = jnp.exp2(m - m_new)
        p = jnp.exp2((s - m_new).astype(BF16))
        return m_new, alpha * acc + _dot(p, v)

    n_blk = seq // tk
    init = (jnp.full((m_rows, 1), -jnp.inf, F32), jnp.zeros((m_rows, LANES), F32))
    _, acc = lax.fori_loop(0, n_blk, body, init, unroll=unroll if n_blk % unroll == 0 else 1)
    o = (acc / acc[:, ONES_LANE:ONES_LANE + 1]).astype(o_ref.dtype)
    for g in range(group):
        o_ref[:, g * LANES:(g + 1) * LANES] = o[g * tq:(g + 1) * tq, :]


def _attention(q, k, v, *, group, tq, tk, unroll):
    bsz, seq, qw = q.shape
    n_kv = k.shape[-1] // LANES
    return pl.pallas_call(
        functools.partial(_attn_kernel, group=group, tk=tk, unroll=unroll),
        out_shape=jax.ShapeDtypeStruct((bsz, seq, qw), BF16),
        grid=(bsz, n_kv, seq // tq),
        in_specs=[
            pl.BlockSpec((None, tq, group * LANES), lambda b, h, i: (b, i, h)),
            pl.BlockSpec((None, seq, LANES), lambda b, h, i: (b, 0, h)),
            pl.BlockSpec((None, seq, LANES), lambda b, h, i: (b, 0, h)),
        ],
        out_specs=pl.BlockSpec((None, tq, group * LANES), lambda b, h, i: (b, i, h)),
        compiler_params=_cparams(("parallel", "parallel", "parallel")),
        name="flash_attn_g%d" % group,
    )(q, k, v)


def _first_argmax(vals):
    best, idx = vals[0], jnp.zeros_like(vals[0], dtype=jnp.int32)
    for i in range(1, len(vals)):
        gt = vals[i] > best
        best = jnp.where(gt, vals[i], best)
        idx = jnp.where(gt, i, idx)
    return best, idx


def _pick(vals, idx):
    out = vals[0]
    for i in range(1, len(vals)):
        out = jnp.where(idx == i, vals[i], out)
    return out


def _outproj_kernel(oa_ref, ob_ref, ga_ref, gb_ref, x_ref, mod_ref, n2_ref,
                    wpa_ref, wpb_ref, wo_ref, wr_ref, rb_ref,
                    x1_ref, h2_ref, bkt_ref, rank_ref, cnt_ref, base_ref, gwt_ref, upper_ref):
    tm = x_ref.shape[0]

    @pl.when((pl.program_id(0) == 0) & (pl.program_id(1) == 0))
    def _():
        base_ref[...] = jnp.zeros_like(base_ref)
        r_i = lax.broadcasted_iota(jnp.int32, (tm, tm), 0)
        c_i = lax.broadcasted_iota(jnp.int32, (tm, tm), 1)
        upper_ref[...] = jnp.where(r_i < c_i, 1.0, 0.0).astype(BF16)

    ya = _dot(oa_ref[...], wpa_ref[...])
    yb = _dot(ob_ref[...], wpb_ref[...])
    merged = ga_ref[...].astype(F32) * ya + gb_ref[...].astype(F32) * yb
    att = _dot(merged.astype(BF16), wo_ref[...])
    g1 = mod_ref[2:3, :]
    sh2 = mod_ref[3:4, :]
    sc2 = mod_ref[4:5, :]
    x1 = x_ref[...] + g1 * att
    x1_ref[...] = x1
    h2 = (_rms(x1, D_MODEL) * n2_ref[...]) * (1.0 + sc2) + sh2
    hi_bits = lax.bitcast_convert_type(h2[:, 0:XS_HALF].astype(BF16).astype(F32), U32)
    lo_bits = lax.bitcast_convert_type(h2[:, XS_HALF:].astype(BF16).astype(F32), U32)
    h2_ref[:, 0:XS_HALF] = hi_bits | (lo_bits >> 16)

    h2_hi, h2_lo = _split(h2)
    both = _dot(h2_hi, wr_ref[...])
    logits = both[:, 0:LANES] + (both[:, LANES:] + _dot(h2_lo, wr_ref[:, 0:LANES]))
    lt = logits.T[0:N_EXPERTS, :]
    scores = jax.nn.sigmoid(lt)
    biased = scores + rb_ref[...]
    brow = [biased[e:e + 1, :] for e in range(N_EXPERTS)]
    srow = [scores[e:e + 1, :] for e in range(N_EXPERTS)]
    grp = []
    for g in range(N_GROUPS):
        a, b, c, d = brow[4 * g:4 * g + 4]
        hi1, lo1 = jnp.maximum(a, b), jnp.minimum(a, b)
        hi2, lo2 = jnp.maximum(c, d), jnp.minimum(c, d)
        grp.append(jnp.maximum(hi1, hi2) + jnp.maximum(jnp.minimum(hi1, hi2),
                                                       jnp.maximum(lo1, lo2)))
    _, gsel = _first_argmax(grp)
    bsel = [_pick([brow[4 * g + i] for g in range(N_GROUPS)], gsel)
            for i in range(EXPERTS_PER_GROUP)]
    ssel = [_pick([srow[4 * g + i] for g in range(N_GROUPS)], gsel)
            for i in range(EXPERTS_PER_GROUP)]
    _, i0 = _first_argmax(bsel)
    _, i1 = _first_argmax([jnp.where(i0 == i, -jnp.inf, bsel[i])
                           for i in range(EXPERTS_PER_GROUP)])
    w0 = _pick(ssel, i0)
    w1 = _pick(ssel, i1)
    wsum = w0 + w1
    first_lo = i0 < i1
    lo = jnp.where(first_lo, i0, i1)
    hi = jnp.where(first_lo, i1, i0)
    w_lo = jnp.where(first_lo, w0, w1) / wsum
    w_hi = jnp.where(first_lo, w1, w0) / wsum
    pair = jnp.where(lo == 0, 0, jnp.where(lo == 1, 3, 5)) + (hi - lo - 1)
    bucket = gsel * N_PAIRS + pair
    bkt_ref[...] = bucket

    gwt_ref[...] = jnp.zeros_like(gwt_ref)
    gwt_ref[0:1, :] = w_lo
    gwt_ref[1:2, :] = w_hi
    h2_ref[:, XS_HALF:] = lax.bitcast_convert_type(gwt_ref[...].T, U32)

    onehot = lax.broadcasted_iota(jnp.int32, (N_BUCKETS, tm), 0) == bucket
    cnt = jnp.where(onehot, 1.0, 0.0)
    pref = _dot(cnt.astype(BF16), upper_ref[...]) + base_ref[:, 0:1]
    rank_ref[...] = jnp.sum(jnp.where(onehot, pref, 0.0), axis=0, keepdims=True).astype(jnp.int32)
    base_ref[...] = base_ref[...] + jnp.sum(cnt, axis=1, keepdims=True)
    cnt_ref[...] = base_ref[...]


def _outproj(oa, ob, ga, gb, x, mod_l, n2, wpa, wpb, wo, wr, rb, tm):
    bsz, seq, d = x.shape
    nt = seq // tm
    row = lambda w: pl.BlockSpec((None, tm, w), lambda b, i: (b, i, 0))
    full = _resident
    tok = pl.BlockSpec((1, tm), lambda b, i: (0, b * nt + i))
    t = bsz * seq
    return pl.pallas_call(
        _outproj_kernel,
        out_shape=[
            jax.ShapeDtypeStruct((bsz, seq, d), F32),
            jax.ShapeDtypeStruct((bsz, seq, XS_COLS), U32),
            jax.ShapeDtypeStruct((1, t), jnp.int32),
            jax.ShapeDtypeStruct((1, t), jnp.int32),
            jax.ShapeDtypeStruct((N_BUCKETS, LANES), F32),
        ],
        grid=(bsz, nt),
        in_specs=[
            row(d), row(d), row(d), row(d), row(d),
            pl.BlockSpec((None, N_MOD, d), lambda b, i: (b, 0, 0)),
            full(n2), full(wpa), full(wpb), full(wo), full(wr), full(rb),
        ],
        out_specs=[row(d), row(XS_COLS), tok, tok,
                   pl.BlockSpec((N_BUCKETS, LANES), lambda b, i: (0, 0))],
        scratch_shapes=[pltpu.VMEM((N_BUCKETS, LANES), F32), pltpu.VMEM((LANES, tm), F32),
                        pltpu.VMEM((tm, tm), BF16)],
        compiler_params=_cparams(("arbitrary", "arbitrary")),
        name="outproj_router",
    )(oa, ob, ga, gb, x, mod_l, n2, wpa, wpb, wo, wr, rb)


def _dispatch_kernel(dest_ref, h_ref, xs_ref, sem):
    tm = h_ref.shape[0]

    def issue(i, c):
        for q in range(2):
            t = 2 * i + q
            pltpu.async_copy(h_ref.at[pl.ds(t, 1)], xs_ref.at[pl.ds(dest_ref[0, t], 1)], sem,
                             priority=q)
        return c

    lax.fori_loop(0, tm // 2, issue, 0, unroll=DMA_UNROLL // 2)
    pltpu.make_async_copy(h_ref, xs_ref.at[pl.ds(0, tm)], sem).wait()


def _dispatch(h2, dest, n_rows, tm):
    t, d = h2.shape
    nt = t // tm
    return pl.pallas_call(
        _dispatch_kernel,
        out_shape=jax.ShapeDtypeStruct((n_rows, d), h2.dtype),
        grid=(nt,),
        in_specs=[
            pl.BlockSpec((None, 1, tm), lambda i: (i, 0, 0), memory_space=pltpu.SMEM),
            pl.BlockSpec((tm, d), lambda i: (i, 0)),
        ],
        out_specs=pl.BlockSpec(memory_space=pl.ANY),
        scratch_shapes=[pltpu.SemaphoreType.DMA],
        compiler_params=_cparams(("arbitrary",)),
        name="moe_dispatch",
    )(dest.reshape(nt, 1, tm), h2)


def _expert_kernel(ea_ref, eb_ref, br_ref, nv_ref, x_ref, w1a_ref, w3a_ref, w2a_ref,
                   w1b_ref, w3b_ref, w2b_ref, y_ref):
    b = pl.program_id(0)

    @pl.when(b < nv_ref[0])
    def _():
        bm = x_ref.shape[0]
        live = lax.broadcasted_iota(jnp.int32, (bm, 1), 0) < br_ref[b]
        words = jnp.where(live, x_ref[:, 0:XS_HALF], 0)
        x = jnp.concatenate(
            [lax.bitcast_convert_type((words >> 16) << 16, F32).astype(BF16),
             lax.bitcast_convert_type(words << 16, F32).astype(BF16)], axis=1)
        gates = lax.bitcast_convert_type(jnp.where(live, x_ref[:, XS_HALF:], 0), F32)

        def hidden(w1_ref, w3_ref, gate):
            a = _dot(x, w1_ref[...])
            g = _dot(x, w3_ref[...])
            return ((a * jax.nn.sigmoid(a)) * g * gate).astype(BF16)

        h_lo = hidden(w1a_ref, w3a_ref, gates[:, 0:1])
        h_hi = hidden(w1b_ref, w3b_ref, gates[:, 1:2])
        y_ref[...] = _dot(h_lo, w2a_ref[...]) + _dot(h_hi, w2b_ref[...])


def _experts(xs, w1, w3, w2, layer, blk_ea, blk_eb, blk_rows, n_valid, bm):
    n_rows = xs.shape[0]
    nb = n_rows // bm
    d, f = w1.shape[-2:]
    last = lambda b, nv: jnp.minimum(b, nv[0] - 1)
    xmap = lambda b, ea, eb, br, nv: (last(b, nv), 0)
    amap = lambda b, ea, eb, br, nv: (layer, ea[last(b, nv)], 0, 0)
    bmap = lambda b, ea, eb, br, nv: (layer, eb[last(b, nv)], 0, 0)
    return pl.pallas_call(
        _expert_kernel,
        out_shape=jax.ShapeDtypeStruct((n_rows, d), F32),
        grid_spec=pltpu.PrefetchScalarGridSpec(
            num_scalar_prefetch=4,
            grid=(nb,),
            in_specs=[
                pl.BlockSpec((bm, XS_COLS), xmap),
                pl.BlockSpec((None, None, d, f), amap),
                pl.BlockSpec((None, None, d, f), amap),
                pl.BlockSpec((None, None, f, d), amap),
                pl.BlockSpec((None, None, d, f), bmap),
                pl.BlockSpec((None, None, d, f), bmap),
                pl.BlockSpec((None, None, f, d), bmap),
            ],
            out_specs=pl.BlockSpec((bm, d), xmap),
        ),
        compiler_params=_cparams(("arbitrary",)),
        name="moe_experts",
    )(blk_ea, blk_eb, blk_rows, n_valid, xs, w1, w3, w2, w1, w3, w2)


def _combine_kernel(dest_ref, x1_ref, mod_ref, nf_ref, ys_ref, o_ref, buf, sem):
    tm = x1_ref.shape[0]

    def issue(t, c):
        pltpu.make_async_copy(ys_ref.at[pl.ds(dest_ref[0, t], 1)], buf.at[pl.ds(t, 1)],
                              sem).start()
        return c

    lax.fori_loop(0, tm, issue, 0, unroll=DMA_UNROLL)
    pltpu.make_async_copy(ys_ref.at[pl.ds(0, tm)], buf, sem).wait()
    x2 = x1_ref[...] + mod_ref[5:6, :] * buf[...]
    o_ref[...] = _rms(x2, D_MODEL) * nf_ref[...]


def _combine(ys, dest, x1, mod_l, nf, tm):
    bsz, seq, d = x1.shape
    nt = seq // tm
    return pl.pallas_call(
        _combine_kernel,
        out_shape=jax.ShapeDtypeStruct((bsz, seq, d), F32),
        grid=(bsz, nt),
        in_specs=[
            pl.BlockSpec((None, 1, tm), lambda b, i: (b * nt + i, 0, 0),
                         memory_space=pltpu.SMEM),
            pl.BlockSpec((None, tm, d), lambda b, i: (b, i, 0)),
            pl.BlockSpec((None, N_MOD, d), lambda b, i: (b, 0, 0)),
            pl.BlockSpec((1, d), lambda b, i: (0, 0)),
            pl.BlockSpec(memory_space=pl.ANY),
        ],
        out_specs=pl.BlockSpec((None, tm, d), lambda b, i: (b, i, 0)),
        scratch_shapes=[pltpu.VMEM((tm, d), F32), pltpu.SemaphoreType.DMA],
        compiler_params=_cparams(("arbitrary", "arbitrary")),
        name="moe_combine",
    )(dest.reshape(bsz * nt, 1, tm), x1, mod_l, nf, ys)


def _rope_tables(seq):
    rows = seq // GRID_W
    row = jnp.broadcast_to(jnp.arange(rows, dtype=F32)[:, None], (rows, GRID_W)).reshape(seq)
    col = jnp.broadcast_to(jnp.arange(GRID_W, dtype=F32)[None, :], (rows, GRID_W)).reshape(seq)

    def cs(rot_dim):
        n = rot_dim // 4
        inv = 1.0 / (ROPE_THETA ** (jnp.arange(n, dtype=F32) / n))
        ang = jnp.concatenate([row[:, None] * inv, col[:, None] * inv], axis=-1)
        return jnp.cos(ang), jnp.sin(ang)

    def place(pieces):
        out = jnp.zeros((seq, LANES), F32)
        for off, val in pieces:
            out = out.at[:, off:off + val.shape[1]].set(val)
        return out

    ca, sa = cs(HEAD_DIM_A)
    ha = HEAD_DIM_A // 2
    scale_a = HEAD_DIM_A ** -0.5 * LOG2E
    ta = jnp.stack([place([(0, ca), (ha, ca)]), place([(0, -sa)]), place([(ha, sa)])]) * scale_a
    cb, sb = cs(QK_ROPE)
    hb = QK_ROPE // 2
    scale_b = (QK_NOPE + QK_ROPE) ** -0.5 * LOG2E
    ones = jnp.ones((seq, QK_NOPE), F32)
    cos_q = place([(0, ones), (QK_NOPE, cb), (QK_NOPE + hb, cb)]) * scale_b
    sin_lo = place([(QK_NOPE, -sb)])
    sin_hi = place([(QK_NOPE + hb, sb)])
    cos_k = place([(QK_NOPE, cb), (QK_NOPE + hb, cb)])
    tb = jnp.stack([cos_q, sin_lo * scale_b, sin_hi * scale_b, cos_k, sin_lo, sin_hi])
    return ta, tb


def _pad_heads(w, n_heads, width, offset=0):
    k = w.shape[0]
    w = w.reshape(k, n_heads, width)
    out = jnp.zeros((k, n_heads, LANES), w.dtype).at[:, :, offset:offset + width].set(w)
    return out.reshape(k, n_heads * LANES)


def _prep_layer(w_in, w_q_b, w_kv_b, w_proj_a, w_proj_b):
    o = 0
    parts = []
    sizes = (N_HEADS_A * HEAD_DIM_A, N_KV_A * HEAD_DIM_A, N_KV_A * HEAD_DIM_A,
             Q_LORA, KV_LORA, QK_ROPE, D_MODEL, D_MODEL)
    segs = []
    for s in sizes:
        segs.append(w_in[:, o:o + s])
        o += s
    parts = [
        segs[0],
        _pad_heads(segs[1], N_KV_A, HEAD_DIM_A),
        _pad_heads(segs[2], N_KV_A, HEAD_DIM_A),
        segs[3], segs[4],
        _pad_heads(segs[5], 1, QK_ROPE, offset=QK_NOPE),
        segs[6], segs[7],
    ]
    win = jnp.concatenate(parts, axis=1).astype(BF16)
    wq = _pad_heads(w_q_b, N_HEADS_B, QK_NOPE + QK_ROPE).astype(BF16)
    wkv = w_kv_b.reshape(KV_LORA, N_HEADS_B, QK_NOPE + V_DIM_B)
    wk = _pad_heads(wkv[:, :, :QK_NOPE].reshape(KV_LORA, -1), N_HEADS_B, QK_NOPE).astype(BF16)
    wv = _pad_heads(wkv[:, :, QK_NOPE:].reshape(KV_LORA, -1), N_HEADS_B, V_DIM_B).astype(BF16)

    def pad_rows(w, n_heads, width):
        return _pad_heads(w.T, n_heads, width).T.astype(BF16)

    wpa = pad_rows(w_proj_a, N_HEADS_A, HEAD_DIM_A)
    wpb = pad_rows(w_proj_b, N_HEADS_B, V_DIM_B)
    return win, wq, wk, wv, wpa, wpb


def _pad_lanes(g, offset=0):
    return jnp.zeros((1, LANES), F32).at[0, offset:offset + g.shape[0]].set(g)


def kernel(x, c, w_ada, b_ada, norm1, w_in, q_norm_a, k_norm_a, q_a_norm, kv_norm, w_q_b, w_kv_b, w_proj_a, w_proj_b, w_o, norm2, w_router, router_bias, w1, w3, w2, norm_f):
    bsz, seq, d = x.shape
    depth = w_in.shape[0]
    t = bsz * seq
    tm = min(256, seq)
    tm_proj = min(512, seq)
    bm = min(256, t)
    tk = min(512, seq)

    mod = _modulation(c, w_ada, b_ada).reshape(depth, bsz, N_MOD, d)
    ta, tb = _rope_tables(seq)
    wr_pad = jnp.zeros((d, LANES), F32).at[:, :N_EXPERTS].set(w_router)
    wr = jnp.concatenate(_split(wr_pad), axis=1)
    rb = router_bias.reshape(N_EXPERTS, 1)
    n_rows = ((t + bm - 1) // bm) * bm + N_BUCKETS * bm
    nb = n_rows // bm
    w1b, w3b, w2b = w1.astype(BF16), w3.astype(BF16), w2.astype(BF16)
    nf = norm_f.reshape(1, d)
    bucket_ids = jnp.arange(N_BUCKETS, dtype=jnp.int32)
    bucket_ea = (bucket_ids // N_PAIRS) * EXPERTS_PER_GROUP + jnp.array(PAIR_LO, jnp.int32)[bucket_ids % N_PAIRS]
    bucket_eb = (bucket_ids // N_PAIRS) * EXPERTS_PER_GROUP + jnp.array(PAIR_HI, jnp.int32)[bucket_ids % N_PAIRS]

    moe = None
    for l in range(depth):
        win, wq, wk, wv, wpa, wpb = _prep_layer(w_in[l], w_q_b[l], w_kv_b[l],
                                                w_proj_a[l], w_proj_b[l])
        outs = _inproj(
            x, mod[l], norm1[l].reshape(1, d), win,
            _pad_lanes(q_norm_a[l]), _pad_lanes(k_norm_a[l]),
            q_a_norm[l].reshape(1, Q_LORA), kv_norm[l].reshape(1, KV_LORA),
            wq, wk, wv, ta, tb, tm_proj, moe=moe)
        qa, ka, va, qb, kb, vb, ga, gb = outs[:8]
        if moe is not None:
            x = outs[8]
        n_unroll = min(8, seq // tk)
        oa = _attention(qa, ka, va, group=GROUP_A, tq=min(256, seq), tk=tk, unroll=n_unroll)
        ob = _attention(qb, kb, vb, group=1, tq=min(1024, seq), tk=tk, unroll=n_unroll)
        x1, h2, bucket, rank, cnt = _outproj(
            oa, ob, ga, gb, x, mod[l], norm2[l].reshape(1, d), wpa, wpb,
            w_o[l].astype(BF16), wr, rb, tm)

        counts = cnt[:, 0].astype(jnp.int32)
        padded = ((counts + bm - 1) // bm) * bm
        pends = jnp.cumsum(padded)
        pstarts = pends - padded
        dest = rank
        for e in range(N_BUCKETS):
            dest = dest + jnp.where(bucket == e, pstarts[e], 0)
        blk_start = jnp.arange(nb, dtype=jnp.int32) * bm
        blk_b = jnp.minimum(jnp.sum(blk_start[:, None] >= pends[None, :], axis=1),
                            N_BUCKETS - 1).astype(jnp.int32)
        blk_rows = jnp.clip(pstarts[blk_b] + counts[blk_b] - blk_start, 0, bm).astype(jnp.int32)
        n_valid = jnp.maximum(pends[-1:] // bm, 1).astype(jnp.int32)

        xs = _dispatch(h2.reshape(t, XS_COLS), dest, n_rows, tm)
        ys = _experts(xs, w1b, w3b, w2b, l, bucket_ea[blk_b], bucket_eb[blk_b], blk_rows,
                      n_valid, bm)
        x, moe = x1, (ys, dest, mod[l])
    return _combine(ys, dest, x1, mod[depth - 1], nf, tm)
```
